```python
import math
import functools
import jax
import jax.numpy as jnp
from jax import lax
import numpy as np

D_MODEL = 2048
BATCH = 8
SEQ = 2048
DEPTH = 2
DEC_BATCH = 32
DEC_SEQ = 1
PAST_LEN = 8192
PAGE_SIZE = 128

SSM_EXPAND = 2
D_INNER = SSM_EXPAND * D_MODEL
SSM_HEAD_DIM = 64
SSM_HEADS = D_INNER // SSM_HEAD_DIM
SSM_GROUPS = 8
SSM_REP = SSM_HEADS // SSM_GROUPS
D_STATE = 128
CONV_W = 4
CONV_DIM = D_INNER + 2 * SSM_GROUPS * D_STATE
SSD_CHUNK = 128
NSA_HEADS = 16
NSA_KV_GROUPS = 4
NSA_REP = NSA_HEADS // NSA_KV_GROUPS
HEAD_DIM = 128
KV_WIDTH = NSA_KV_GROUPS * HEAD_DIM
ROT_DIM = HEAD_DIM // 4
ROPE_THETA = 500000.0
ATTN_SCALE = HEAD_DIM ** -0.5
CMP_STRIDE = 16
CMP_LEN = 2 * CMP_STRIDE
PHI_HIDDEN = 128
SEL_BLOCK = 64
N_SEL = 16
WINDOW = 512
WIN_QBLOCK = 128
SLC_QCHUNK = 16
FORCE_BONUS = 1.0e4
NEG_INF = -1.0e30
D_FF = 5632
N_EXPERTS = 8
TOP_K = 2
D_FF_EXPERT = 7168
N_DENSE = (DEPTH + 1) // 2
N_MOE = DEPTH // 2
RMS_EPS = 1e-6
IN_SIZES = (D_INNER, CONV_DIM, SSM_HEADS, NSA_HEADS * HEAD_DIM, 6 * KV_WIDTH, 3 * NSA_HEADS, 2 * D_MODEL)
IN_COLS = sum(IN_SIZES)

kernel_name = 'hybrid_ssd_nsa_moe_decode_step'

F32 = jnp.float32


def rmsnorm(x, w):
    xf = x.astype(F32)
    y = xf * lax.rsqrt(jnp.mean(xf * xf, axis=-1, keepdims=True) + RMS_EPS)
    return (y * w.astype(F32)).astype(x.dtype)


def masked_softmax(s, mask):
    p = jax.nn.softmax(jnp.where(mask, s, NEG_INF), axis=-1)
    return p * mask


def rotary(x, pos):
    half = ROT_DIM // 2
    inv = ROPE_THETA ** (-jnp.arange(half, dtype=F32) / half)
    ang = pos.astype(F32)[:, None] * inv[None, :]
    ang = ang.reshape((1, pos.shape[0]) + (1,) * (x.ndim - 3) + (half,))
    cos, sin = jnp.cos(ang), jnp.sin(ang)
    xr = x[..., :ROT_DIM].astype(F32)
    x1, x2 = xr[..., :half], xr[..., half:]
    rot = jnp.concatenate([x1 * cos - x2 * sin, x2 * cos + x1 * sin], axis=-1)
    return jnp.concatenate([rot.astype(x.dtype), x[..., ROT_DIM:]], axis=-1)


def split_in(proj):
    cuts = np.cumsum(IN_SIZES)[:-1].tolist()
    return jnp.split(proj, cuts, axis=-1)


def ssd_chunked(x, dt, a, bm, cm, h0):
    b, T = x.shape[:2]
    q = SSD_CHUNK if T >= SSD_CHUNK else T
    nc = -(-T // q)
    pad = nc * q - T

    def chunks(v):
        v = jnp.pad(v.astype(F32), [(0, 0), (0, pad)] + [(0, 0)] * (v.ndim - 2))
        return jnp.moveaxis(v.reshape((b, nc, q) + v.shape[2:]), 1, 0)

    causal = jnp.tril(jnp.ones((q, q), bool))[None, :, :, None, None]

    def step(h, inp):
        xc, dtc, bc, cc = inp
        acum = jnp.cumsum(dtc * a, axis=1)
        seg = acum[:, :, None] - acum[:, None, :]
        lmat = jnp.exp(jnp.where(causal, seg, -jnp.inf))
        cb = jnp.einsum('blgn,bsgn->blsg', cc, bc)
        y_diag = jnp.einsum('blsg,blsgr,bsgrp->blgrp', cb, lmat, dtc[..., None] * xc)
        y_off = jnp.einsum('blgn,bgrpn->blgrp', cc, h) * jnp.exp(acum)[..., None]
        decay = jnp.exp(acum[:, -1:] - acum) * dtc
        h_new = jnp.exp(acum[:, -1])[..., None, None] * h + jnp.einsum('bsgn,bsgr,bsgrp->bgrpn', bc, decay, xc)
        return h_new, y_diag + y_off

    h_fin, ys = lax.scan(step, h0.astype(F32), (chunks(x), chunks(dt), chunks(bm), chunks(cm)))
    y = jnp.moveaxis(ys, 0, 1).reshape((b, nc * q) + x.shape[2:])[:, :T]
    return y, h_fin


def ssm_branch(z, xbc, dt_raw, conv0, h0, conv_w, conv_b, dt_bias, a_log, d_skip, norm_w):
    b, T, _ = xbc.shape
    xpad = jnp.concatenate([conv0.astype(xbc.dtype), xbc], axis=1)
    acc = conv_b
    for k in range(CONV_W):
        acc = acc + xpad[:, k:k + T] * conv_w[k]
    xbc_c = jax.nn.silu(acc)
    new_conv = xpad[:, T:]
    xs, bm, cm = jnp.split(xbc_c, [D_INNER, D_INNER + SSM_GROUPS * D_STATE], axis=-1)
    xs = xs.reshape(b, T, SSM_GROUPS, SSM_REP, SSM_HEAD_DIM)
    bm = bm.reshape(b, T, SSM_GROUPS, D_STATE)
    cm = cm.reshape(b, T, SSM_GROUPS, D_STATE)
    dt = jax.nn.softplus(dt_raw.astype(F32) + dt_bias.astype(F32)).reshape(b, T, SSM_GROUPS, SSM_REP)
    a = -jnp.exp(a_log.astype(F32)).reshape(SSM_GROUPS, SSM_REP)
    h0 = h0.reshape(b, SSM_GROUPS, SSM_REP, SSM_HEAD_DIM, D_STATE)
    y, h_fin = ssd_chunked(xs, dt, a, bm, cm, h0)
    y = y + d_skip.astype(F32).reshape(SSM_GROUPS, SSM_REP)[..., None] * xs.astype(F32)
    y = y.reshape(b, T, D_INNER) * jax.nn.silu(z.astype(F32))
    yg = y.reshape(b, T, SSM_GROUPS, D_INNER // SSM_GROUPS)
    yg = yg * lax.rsqrt(jnp.mean(yg * yg, axis=-1, keepdims=True) + RMS_EPS)
    y = yg.reshape(b, T, D_INNER) * norm_w.astype(F32)
    return y.astype(z.dtype), new_conv, h_fin.reshape(b, SSM_HEADS, SSM_HEAD_DIM, D_STATE)


def compress(rows, w1, w2, pe):
    b, L = rows.shape[:2]
    nf = L // CMP_STRIDE
    ch = rows[:, :nf * CMP_STRIDE].reshape(b, nf, CMP_STRIDE, NSA_KV_GROUPS, HEAD_DIM)
    first = jnp.einsum('bnsgd,sde->bnge', ch, w1[:CMP_STRIDE])
    second = jnp.einsum('bnsgd,sde->bnge', ch, w1[CMP_STRIDE:])
    pe_bias = jnp.einsum('ld,lde->e', pe, w1)
    hid = jax.nn.gelu(first[:, :-1] + second[:, 1:] + pe_bias)
    return jnp.einsum('bnge,ed->bngd', hid, w2)


def cmp_to_sel(n_cb, n_sb):
    i = jnp.arange(n_cb)[:, None] * CMP_STRIDE
    j = jnp.arange(n_sb)[None, :] * SEL_BLOCK
    return ((i < j + SEL_BLOCK) & (i + CMP_LEN > j)).astype(F32)


def cmp_attend(q, kc, vc, q_pos):
    n = kc.shape[1]
    s = jnp.einsum('btgrd,bngd->bgrtn', q, kc, preferred_element_type=F32) * ATTN_SCALE
    end = jnp.arange(n) * CMP_STRIDE + CMP_LEN - 1
    p = masked_softmax(s, end[None, :] <= q_pos[:, None])
    o = jnp.einsum('bgrtn,bngd->btgrd', p.astype(vc.dtype), vc)
    return o, p


def select_blocks(imp, q_pos):
    n_sb = imp.shape[-1]
    j = jnp.arange(n_sb)[None, :]
    cur = (q_pos // SEL_BLOCK)[:, None]
    valid = (j * SEL_BLOCK <= q_pos[:, None])[:, None, :]
    forced = ((j == 0) | (j == cur) | (j == cur - 1))[:, None, :]
    score = jnp.where(valid, imp + jnp.where(forced, FORCE_BONUS, 0.0), NEG_INF)
    top, idx = lax.top_k(score, min(N_SEL, n_sb))
    return idx, top > 0.5 * NEG_INF


def slc_attend(q, kb, vb, idx, sel_ok, q_pos):
    kpos = idx[..., None] * SEL_BLOCK + jnp.arange(SEL_BLOCK)
    mask = sel_ok[..., None] & (kpos <= q_pos[None, :, None, None, None])
    s = jnp.einsum('btgrd,btgkld->btgrkl', q, kb, preferred_element_type=F32) * ATTN_SCALE
    b, T, G, R, K, L = s.shape
    p = masked_softmax(s.reshape(b, T, G, R, K * L), mask.reshape(b, T, G, 1, K * L))
    return jnp.einsum('btgrkl,btgkld->btgrd', p.reshape(s.shape).astype(vb.dtype), vb)


def slc_prompt(q, k, v, idx, sel_ok, q_pos):
    b, T = q.shape[:2]
    nsb = T // SEL_BLOCK
    kblk = k.reshape(b, nsb, SEL_BLOCK, NSA_KV_GROUPS, HEAD_DIM)
    vblk = v.reshape(b, nsb, SEL_BLOCK, NSA_KV_GROUPS, HEAD_DIM)
    bi = jnp.arange(b)[:, None, None, None]
    gi = jnp.arange(NSA_KV_GROUPS)[None, None, :, None]
    nq = T // SLC_QCHUNK

    def chunked(a):
        return jnp.moveaxis(a.reshape((b, nq, SLC_QCHUNK) + a.shape[2:]), 1, 0)

    def body(args):
        qc, ic, okc, pc = args
        kb = kblk[bi, ic, :, gi, :]
        vb = vblk[bi, ic, :, gi, :]
        return slc_attend(qc, kb, vb, ic, okc, pc)

    out = lax.map(body, (chunked(q), chunked(idx), chunked(sel_ok), q_pos.reshape(nq, SLC_QCHUNK)))
    return jnp.moveaxis(out, 0, 1).reshape(q.shape)


def gather_selected(pool, new_rows, idx, page_table, past):
    b, S = new_rows.shape[:2]
    bpp = PAGE_SIZE // SEL_BLOCK
    n_past_blk = past // SEL_BLOCK
    n_new_blk = -(-S // SEL_BLOCK)
    bi = jnp.arange(b)[:, None, None, None]
    gi = jnp.arange(NSA_KV_GROUPS)[None, None, :, None]
    jp = jnp.minimum(idx, n_past_blk - 1)
    phys = page_table[bi, jp // bpp]
    pool_blk = pool.reshape(pool.shape[0], bpp, SEL_BLOCK, NSA_KV_GROUPS, HEAD_DIM)
    from_past = pool_blk[phys, jp % bpp, :, gi, :]
    new_blk = jnp.pad(new_rows, ((0, 0), (0, n_new_blk * SEL_BLOCK - S), (0, 0), (0, 0)))
    new_blk = new_blk.reshape(b, n_new_blk, SEL_BLOCK, NSA_KV_GROUPS, HEAD_DIM)
    jn = jnp.clip(idx - n_past_blk, 0, n_new_blk - 1)
    from_new = new_blk[bi, jn, :, gi, :]
    return jnp.where((idx < n_past_blk)[..., None, None], from_past, from_new.astype(from_past.dtype))


def window_attend(q, k, v, q_pos, k_pos):
    s = jnp.einsum('btgrd,bsgd->bgrts', q, k, preferred_element_type=F32) * ATTN_SCALE
    diff = q_pos[:, None] - k_pos[None, :]
    mask = (diff >= 0) & (diff < WINDOW) & (k_pos[None, :] >= 0)
    p = masked_softmax(s, mask)
    return jnp.einsum('bgrts,bsgd->btgrd', p.astype(v.dtype), v)


def win_prompt(q, k, v):
    b, T = q.shape[:2]
    nq = T // WIN_QBLOCK
    span = WIN_QBLOCK + WINDOW
    kp = jnp.pad(k, ((0, 0), (WINDOW, 0), (0, 0), (0, 0)))
    vp = jnp.pad(v, ((0, 0), (WINDOW, 0), (0, 0), (0, 0)))
    qb = jnp.moveaxis(q.reshape((b, nq, WIN_QBLOCK) + q.shape[2:]), 1, 0)

    def body(args):
        i, qc = args
        start = i * WIN_QBLOCK
        kc = lax.dynamic_slice_in_dim(kp, start, span, axis=1)
        vc = lax.dynamic_slice_in_dim(vp, start, span, axis=1)
        q_pos = start + jnp.arange(WIN_QBLOCK)
        k_pos = start - WINDOW + jnp.arange(span)
        return window_attend(qc, kc, vc, q_pos, k_pos)

    out = lax.map(body, (jnp.arange(nq), qb))
    return jnp.moveaxis(out, 0, 1).reshape(q.shape)


def combine_nsa(gate, o_cmp, o_slc, o_win, dtype):
    b, T = o_cmp.shape[:2]
    o = gate[:, :, 0] * o_cmp + gate[:, :, 1] * o_slc + gate[:, :, 2] * o_win
    return o.reshape(b, T, NSA_HEADS * HEAD_DIM).astype(dtype)


def nsa_prompt(q, kv, gate, phi):
    phi1_k, phi2_k, pe_k, phi1_v, phi2_v, pe_v = phi
    b, T = q.shape[:2]
    pos = jnp.arange(T, dtype=jnp.int32)
    q = rotary(q, pos)
    k_cmp, k_slc, k_win = rotary(kv[:, :, 0], pos), rotary(kv[:, :, 2], pos), rotary(kv[:, :, 4], pos)
    v_cmp, v_slc, v_win = kv[:, :, 1], kv[:, :, 3], kv[:, :, 5]
    kc = compress(k_cmp, phi1_k, phi2_k, pe_k)
    vc = compress(v_cmp, phi1_v, phi2_v, pe_v)
    o_cmp, p_cmp = cmp_attend(q, kc, vc, pos)
    imp = jnp.einsum('bgrtn,nj->btgj', p_cmp, cmp_to_sel(kc.shape[1], -(-T // SEL_BLOCK)))
    idx, sel_ok = select_blocks(imp, pos)
    o_slc = slc_prompt(q, k_slc, v_slc, idx, sel_ok, pos)
    o_win = win_prompt(q, k_win, v_win)
    wb = min(WINDOW, T)
    o = combine_nsa(gate, o_cmp, o_slc, o_win, q.dtype)
    return o, (k_cmp, v_cmp, k_slc, v_slc, k_win[:, T - wb:], v_win[:, T - wb:])


def nsa_sample(q, kv, gate, phi, ck_cmp, cv_cmp, ck_slc, cv_slc, ck_win, cv_win, page_table):
    phi1_k, phi2_k, pe_k, phi1_v, phi2_v, pe_v = phi
    b, S = q.shape[:2]
    past = page_table.shape[1] * PAGE_SIZE
    pos = past + jnp.arange(S, dtype=jnp.int32)
    q = rotary(q, pos)
    k_cmp, k_slc, k_win = rotary(kv[:, :, 0], pos), rotary(kv[:, :, 2], pos), rotary(kv[:, :, 4], pos)
    v_cmp, v_slc, v_win = kv[:, :, 1], kv[:, :, 3], kv[:, :, 5]

    def gather_past(pool):
        return pool[page_table].reshape(b, past, NSA_KV_GROUPS, HEAD_DIM).astype(k_cmp.dtype)

    kc = compress(jnp.concatenate([gather_past(ck_cmp), k_cmp], axis=1), phi1_k, phi2_k, pe_k)
    vc = compress(jnp.concatenate([gather_past(cv_cmp), v_cmp], axis=1), phi1_v, phi2_v, pe_v)
    o_cmp, p_cmp = cmp_attend(q, kc, vc, pos)
    imp = jnp.einsum('bgrtn,nj->btgj', p_cmp, cmp_to_sel(kc.shape[1], -(-(past + S) // SEL_BLOCK)))
    idx, sel_ok = select_blocks(imp, pos)
    kb = gather_selected(ck_slc, k_slc, idx, page_table, past)
    vb = gather_selected(cv_slc, v_slc, idx, page_table, past)
    o_slc = slc_attend(q, kb, vb, idx, sel_ok, pos)
    wb = ck_win.shape[1]
    k_all = jnp.concatenate([ck_win.astype(k_win.dtype), k_win], axis=1)
    v_all = jnp.concatenate([cv_win.astype(v_win.dtype), v_win], axis=1)
    k_pos = past - wb + jnp.arange(wb + S)
    o_win = window_attend(q, k_all, v_all, pos, k_pos)
    o = combine_nsa(gate, o_cmp, o_slc, o_win, q.dtype)
    return o, (k_cmp, v_cmp, k_slc, v_slc, k_all[:, -wb:], v_all[:, -wb:])


def token_mixer(h, conv0, ssm0, nsa_fn, norm_w, w_in, conv_w, conv_b, dt_bias, a_log, d_skip,
                ssm_norm_w, w_br_ssm, w_br_nsa, w_out):
    b, T, _ = h.shape
    u = rmsnorm(h, norm_w)
    z, xbc, dt_raw, q, kv, g_nsa, g_merge = split_in(u @ w_in)
    y_ssm, conv_new, ssm_new = ssm_branch(z, xbc, dt_raw, conv0, ssm0, conv_w, conv_b,
                                          dt_bias, a_log, d_skip, ssm_norm_w)
    q = q.reshape(b, T, NSA_KV_GROUPS, NSA_REP, HEAD_DIM)
    kv = kv.reshape(b, T, 6, NSA_KV_GROUPS, HEAD_DIM)
    gate = jax.nn.sigmoid(g_nsa.astype(F32)).reshape(b, T, 3, NSA_KV_GROUPS, NSA_REP, 1)
    o_nsa, nsa_state = nsa_fn(q, kv, gate)
    gm = jax.nn.sigmoid(g_merge.astype(F32)).reshape(b, T, 2, D_MODEL)
    merged = gm[:, :, 0] * (y_ssm @ w_br_ssm).astype(F32) + gm[:, :, 1] * (o_nsa @ w_br_nsa).astype(F32)
    out = merged.astype(h.dtype) @ w_out
    return out, nsa_state + (ssm_new.astype(h.dtype), conv_new)


def swiglu(u, wg, wu, wd):
    return (jax.nn.silu(u @ wg) * (u @ wu)) @ wd


def moe(u, w_router, wg, wu, wd):
    logits = (u @ w_router).astype(F32)
    top_l, top_i = lax.top_k(logits, TOP_K)
    top_w = jax.nn.softmax(top_l, axis=-1)
    gates = jnp.sum(jax.nn.one_hot(top_i, N_EXPERTS, dtype=F32) * top_w[..., None], axis=-2)
    y = jnp.zeros(u.shape[:-1] + (D_MODEL,), F32)
    for e in range(N_EXPERTS):
        y = y + gates[..., e:e + 1] * swiglu(u, wg[e], wu[e], wd[e]).astype(F32)
    return y.astype(u.dtype)


def channel_mixer(h, l, norm_w, w_gate, w_up, w_down, w_router, w_gate_e, w_up_e, w_down_e):
    u = rmsnorm(h, norm_w)
    i = l // 2
    if l % 2 == 0:
        return swiglu(u, w_gate[i], w_up[i], w_down[i])
    return moe(u, w_router[i], w_gate_e[i], w_up_e[i], w_down_e[i])


def setup_inputs(seed: int = 0) -> dict:
    key = jax.random.key(seed)
    ks = list(jax.random.split(key, 40))

    def nrm(shape, scale):
        return scale * jax.random.normal(ks.pop(), shape, F32)

    def gain(shape):
        return 1.0 + nrm(shape, 0.01)

    n_pages = PAST_LEN // PAGE_SIZE
    n_used = DEC_BATCH * n_pages
    n_pool = n_used + max(n_used // 4, 1)
    win_buf = min(WINDOW, PAST_LEN)
    pool_shape = (DEPTH, n_pool, PAGE_SIZE, NSA_KV_GROUPS, HEAD_DIM)
    win_shape = (DEPTH, DEC_BATCH, win_buf, NSA_KV_GROUPS, HEAD_DIM)
    page_table = jax.random.permutation(ks.pop(), n_pool)[:n_used].reshape(DEC_BATCH, n_pages).astype(jnp.int32)
    dt0 = jnp.exp(jax.random.uniform(ks.pop(), (DEPTH, SSM_HEADS), F32, math.log(1e-3), math.log(1e-1)))
    dt_bias = dt0 + jnp.log(-jnp.expm1(-dt0))
    a_log = jnp.log(jax.random.uniform(ks.pop(), (DEPTH, SSM_HEADS), F32, 1.0, 16.0))
    return {
        'x_prompt': nrm((BATCH, SEQ, D_MODEL), 1.0),
        'x_sample': nrm((DEC_BATCH, DEC_SEQ, D_MODEL), 1.0),
        'cache_k_cmp': nrm(pool_shape, 1.0),
        'cache_v_cmp': nrm(pool_shape, 1.0),
        'cache_k_slc': nrm(pool_shape, 1.0),
        'cache_v_slc': nrm(pool_shape, 1.0),
        'cache_k_win': nrm(win_shape, 1.0),
        'cache_v_win': nrm(win_shape, 1.0),
        'state_ssm': nrm((DEPTH, DEC_BATCH, SSM_HEADS, SSM_HEAD_DIM, D_STATE), 0.1),
        'state_conv': nrm((DEPTH, DEC_BATCH, CONV_W - 1, CONV_DIM), 1.0),
        'page_table': page_table,
        'norm_mix': gain((DEPTH, D_MODEL)),
        'w_in': nrm((DEPTH, D_MODEL, IN_COLS), D_MODEL ** -0.5),
        'conv_w': nrm((DEPTH, CONV_W, CONV_DIM), CONV_W ** -0.5),
        'conv_b': nrm((DEPTH, CONV_DIM), 0.02),
        'dt_bias': dt_bias,
        'a_log': a_log,
        'd_skip': 1.0 + nrm((DEPTH, SSM_HEADS), 0.1),
        'ssm_norm': gain((DEPTH, D_INNER)),
        'phi1_k': nrm((DEPTH, CMP_LEN, HEAD_DIM, PHI_HIDDEN), (CMP_LEN * HEAD_DIM) ** -0.5),
        'phi2_k': nrm((DEPTH, PHI_HIDDEN, HEAD_DIM), PHI_HIDDEN ** -0.5),
        'pe_k': nrm((DEPTH, CMP_LEN, HEAD_DIM), 0.02),
        'phi1_v': nrm((DEPTH, CMP_LEN, HEAD_DIM, PHI_HIDDEN), (CMP_LEN * HEAD_DIM) ** -0.5),
        'phi2_v': nrm((DEPTH, PHI_HIDDEN, HEAD_DIM), PHI_HIDDEN ** -0.5),
        'pe_v': nrm((DEPTH, CMP_LEN, HEAD_DIM), 0.02),
        'w_br_ssm': nrm((DEPTH, D_INNER, D_MODEL), D_INNER ** -0.5),
        'w_br_nsa': nrm((DEPTH, NSA_HEADS * HEAD_DIM, D_MODEL), (NSA_HEADS * HEAD_DIM) ** -0.5),
        'w_out': nrm((DEPTH, D_MODEL, D_MODEL), D_MODEL ** -0.5),
        'norm_ffn': gain((DEPTH, D_MODEL)),
        'w_gate': nrm((N_DENSE, D_MODEL, D_FF), D_MODEL ** -0.5),
        'w_up': nrm((N_DENSE, D_MODEL, D_FF), D_MODEL ** -0.5),
        'w_down': nrm((N_DENSE, D_FF, D_MODEL), D_FF ** -0.5),
        'w_router': nrm((N_MOE, D_MODEL, N_EXPERTS), D_MODEL ** -0.5),
        'w_gate_e': nrm((N_MOE, N_EXPERTS, D_MODEL, D_FF_EXPERT), D_MODEL ** -0.5),
        'w_up_e': nrm((N_MOE, N_EXPERTS, D_MODEL, D_FF_EXPERT), D_MODEL ** -0.5),
        'w_down_e': nrm((N_MOE, N_EXPERTS, D_FF_EXPERT, D_MODEL), D_FF_EXPERT ** -0.5),
        'norm_final': gain((D_MODEL,)),
    }


def reference(x_prompt, x_sample, cache_k_cmp, cache_v_cmp, cache_k_slc, cache_v_slc,
              cache_k_win, cache_v_win, state_ssm, state_conv, page_table,
              norm_mix, w_in, conv_w, conv_b, dt_bias, a_log, d_skip, ssm_norm,
              phi1_k, phi2_k, pe_k, phi1_v, phi2_v, pe_v,
              w_br_ssm, w_br_nsa, w_out, norm_ffn,
              w_gate, w_up, w_down, w_router, w_gate_e, w_up_e, w_down_e, norm_final):
    hp, hs = x_prompt, x_sample
    bp = x_prompt.shape[0]
    p_states = [[] for _ in range(8)]
    s_states = [[] for _ in range(8)]
    for l in range(DEPTH):
        mw = (norm_mix[l], w_in[l], conv_w[l], conv_b[l], dt_bias[l], a_log[l], d_skip[l],
              ssm_norm[l], w_br_ssm[l], w_br_nsa[l], w_out[l])
        phi = (phi1_k[l], phi2_k[l], pe_k[l], phi1_v[l], phi2_v[l], pe_v[l])
        ffn = (w_gate, w_up, w_down, w_router, w_gate_e, w_up_e, w_down_e)
        conv0 = jnp.zeros((bp, CONV_W - 1, CONV_DIM), hp.dtype)
        ssm0 = jnp.zeros((bp, SSM_HEADS, SSM_HEAD_DIM, D_STATE), F32)
        out, st = token_mixer(hp, conv0, ssm0, functools.partial(nsa_prompt, phi=phi), *mw)
        hp = hp + out
        hp = hp + channel_mixer(hp, l, norm_ffn[l], *ffn)
        for i in range(8):
            p_states[i].append(st[i])
        nsa_fn = functools.partial(nsa_sample, phi=phi, ck_cmp=cache_k_cmp[l], cv_cmp=cache_v_cmp[l],
                                   ck_slc=cache_k_slc[l], cv_slc=cache_v_slc[l],
                                   ck_win=cache_k_win[l], cv_win=cache_v_win[l], page_table=page_table)
        out, st = token_mixer(hs, state_conv[l], state_ssm[l], nsa_fn, *mw)
        hs = hs + out
        hs = hs + channel_mixer(hs, l, norm_ffn[l], *ffn)
        for i in range(8):
            s_states[i].append(st[i])
    y_prompt = rmsnorm(hp, norm_final)
    y_sample = rmsnorm(hs, norm_final)
    p_k_cmp, p_v_cmp, p_k_slc, p_v_slc, p_k_win, p_v_win, p_ssm, p_conv = [jnp.stack(a) for a in p_states]
    s_k_cmp, s_v_cmp, s_k_slc, s_v_slc, s_k_win, s_v_win, s_ssm, s_conv = [jnp.stack(a) for a in s_states]
    return (y_prompt, y_sample,
            p_k_cmp, p_v_cmp, p_k_slc, p_v_slc, p_k_win, p_v_win, p_ssm, p_conv,
            s_k_cmp, s_v_cmp, s_k_slc, s_v_slc, s_k_win, s_v_win, s_ssm, s_conv)
```

```python
import functools
import math

import jax
import jax.numpy as jnp
import numpy as np
from jax import lax
from jax.experimental import pallas as pl
from jax.experimental.pallas import tpu as pltpu

F32 = jnp.float32
BF16 = jnp.bfloat16

D_MODEL = 2048
DEPTH = 2
PAGE_SIZE = 128
D_INNER = 4096
SSM_HEAD_DIM = 64
SSM_HEADS = 64
SSM_GROUPS = 8
SSM_REP = 8
D_STATE = 128
CONV_W = 4
CONV_DIM = D_INNER + 2 * SSM_GROUPS * D_STATE
SSD_CHUNK = 128
NSA_HEADS = 16
NSA_KV_GROUPS = 4
NSA_REP = 4
HEAD_DIM = 128
KV_WIDTH = NSA_KV_GROUPS * HEAD_DIM
ROT_DIM = HEAD_DIM // 4
ROPE_THETA = 500000.0
ATTN_SCALE = HEAD_DIM ** -0.5
CMP_STRIDE = 16
CMP_LEN = 32
SEL_BLOCK = 64
N_SEL = 16
WINDOW = 512
WIN_QBLOCK = 128
SLC_QCHUNK = 16
FORCE_BONUS = 1.0e4
NEG_INF = -1.0e30
N_EXPERTS = 8
TOP_K = 2
RMS_EPS = 1e-6
IN_SIZES = (D_INNER, CONV_DIM, SSM_HEADS, NSA_HEADS * HEAD_DIM, 6 * KV_WIDTH, 3 * NSA_HEADS, 2 * D_MODEL)

V7X_VMEM_LIMIT_BYTES = 56 * 1024 * 1024
LANE = 128


def _mm_kernel(x_ref, w_ref, o_ref, acc_ref):
    k = pl.program_id(2)

    @pl.when(k == 0)
    def _():
        acc_ref[...] = jnp.zeros_like(acc_ref)

    acc_ref[...] += jnp.dot(x_ref[...].astype(BF16), w_ref[...].astype(BF16),
                            preferred_element_type=F32)

    @pl.when(k == pl.num_programs(2) - 1)
    def _():
        o_ref[...] = acc_ref[...]


def _pick_tk(K, cap=2048):
    best = None
    for t in range(LANE, min(K, cap) + 1, LANE):
        if K % t == 0:
            best = t
    return best if best is not None else K


def pmm(x, w, tm=1024, tn=512):
    M, K = x.shape
    K2, N = w.shape
    assert K == K2
    tm = min(tm, M)
    tn = min(tn, N)
    tk = _pick_tk(K)
    assert M % tm == 0
    grid = (M // tm, pl.cdiv(N, tn), K // tk)
    return pl.pallas_call(
        _mm_kernel,
        grid=grid,
        in_specs=[pl.BlockSpec((tm, tk), lambda i, j, k: (i, k)),
                  pl.BlockSpec((tk, tn), lambda i, j, k: (k, j))],
        out_specs=pl.BlockSpec((tm, tn), lambda i, j, k: (i, j)),
        out_shape=jax.ShapeDtypeStruct((M, N), F32),
        scratch_shapes=[pltpu.VMEM((tm, tn), F32)],
        compiler_params=pltpu.CompilerParams(
            dimension_semantics=("parallel", "parallel", "arbitrary"),
            vmem_limit_bytes=V7X_VMEM_LIMIT_BYTES),
        name="pmm",
    )(x, w)


def mm(x, w):
    lead = x.shape[:-1]
    return pmm(x.reshape(-1, x.shape[-1]), w).reshape(lead + (w.shape[-1],))


def rmsnorm(x, w):
    xf = x.astype(F32)
    y = xf * lax.rsqrt(jnp.mean(xf * xf, axis=-1, keepdims=True) + RMS_EPS)
    return (y * w.astype(F32)).astype(x.dtype)


def masked_softmax(s, mask):
    p = jax.nn.softmax(jnp.where(mask, s, NEG_INF), axis=-1)
    return p * mask


def rotary(x, pos):
    half = ROT_DIM // 2
    inv = ROPE_THETA ** (-jnp.arange(half, dtype=F32) / half)
    ang = pos.astype(F32)[:, None] * inv[None, :]
    ang = ang.reshape((1, pos.shape[0]) + (1,) * (x.ndim - 3) + (half,))
    cos, sin = jnp.cos(ang), jnp.sin(ang)
    xr = x[..., :ROT_DIM].astype(F32)
    x1, x2 = xr[..., :half], xr[..., half:]
    rot = jnp.concatenate([x1 * cos - x2 * sin, x2 * cos + x1 * sin], axis=-1)
    return jnp.concatenate([rot.astype(x.dtype), x[..., ROT_DIM:]], axis=-1)


def split_in(proj):
    cuts = np.cumsum(IN_SIZES)[:-1].tolist()
    return jnp.split(proj, cuts, axis=-1)


def ssd_chunked(x, dt, a, bm, cm, h0):
    b, T = x.shape[:2]
    q = SSD_CHUNK if T >= SSD_CHUNK else T
    nc = -(-T // q)
    pad = nc * q - T

    def chunks(v):
        v = jnp.pad(v.astype(F32), [(0, 0), (0, pad)] + [(0, 0)] * (v.ndim - 2))
        return jnp.moveaxis(v.reshape((b, nc, q) + v.shape[2:]), 1, 0)

    causal = jnp.tril(jnp.ones((q, q), bool))[None, :, :, None, None]

    def step(h, inp):
        xc, dtc, bc, cc = inp
        acum = jnp.cumsum(dtc * a, axis=1)
        seg = acum[:, :, None] - acum[:, None, :]
        lmat = jnp.exp(jnp.where(causal, seg, -jnp.inf))
        cb = jnp.einsum('blgn,bsgn->blsg', cc, bc)
        y_diag = jnp.einsum('blsg,blsgr,bsgrp->blgrp', cb, lmat, dtc[..., None] * xc)
        y_off = jnp.einsum('blgn,bgrpn->blgrp', cc, h) * jnp.exp(acum)[..., None]
        decay = jnp.exp(acum[:, -1:] - acum) * dtc
        h_new = jnp.exp(acum[:, -1])[..., None, None] * h + jnp.einsum('bsgn,bsgr,bsgrp->bgrpn', bc, decay, xc)
        return h_new, y_diag + y_off

    h_fin, ys = lax.scan(step, h0.astype(F32), (chunks(x), chunks(dt), chunks(bm), chunks(cm)))
    y = jnp.moveaxis(ys, 0, 1).reshape((b, nc * q) + x.shape[2:])[:, :T]
    return y, h_fin


def ssm_branch(z, xbc, dt_raw, conv0, h0, conv_w, conv_b, dt_bias, a_log, d_skip, norm_w):
    b, T, _ = xbc.shape
    xpad = jnp.concatenate([conv0.astype(xbc.dtype), xbc], axis=1)
    acc = conv_b
    for k in range(CONV_W):
        acc = acc + xpad[:, k:k + T] * conv_w[k]
    xbc_c = jax.nn.silu(acc)
    new_conv = xpad[:, T:]
    xs, bm, cm = jnp.split(xbc_c, [D_INNER, D_INNER + SSM_GROUPS * D_STATE], axis=-1)
    xs = xs.reshape(b, T, SSM_GROUPS, SSM_REP, SSM_HEAD_DIM)
    bm = bm.reshape(b, T, SSM_GROUPS, D_STATE)
    cm = cm.reshape(b, T, SSM_GROUPS, D_STATE)
    dt = jax.nn.softplus(dt_raw.astype(F32) + dt_bias.astype(F32)).reshape(b, T, SSM_GROUPS, SSM_REP)
    a = -jnp.exp(a_log.astype(F32)).reshape(SSM_GROUPS, SSM_REP)
    h0 = h0.reshape(b, SSM_GROUPS, SSM_REP, SSM_HEAD_DIM, D_STATE)
    y, h_fin = ssd_chunked(xs, dt, a, bm, cm, h0)
    y = y + d_skip.astype(F32).reshape(SSM_GROUPS, SSM_REP)[..., None] * xs.astype(F32)
    y = y.reshape(b, T, D_INNER) * jax.nn.silu(z.astype(F32))
    yg = y.reshape(b, T, SSM_GROUPS, D_INNER // SSM_GROUPS)
    yg = yg * lax.rsqrt(jnp.mean(yg * yg, axis=-1, keepdims=True) + RMS_EPS)
    y = yg.reshape(b, T, D_INNER) * norm_w.astype(F32)
    return y.astype(z.dtype), new_conv, h_fin.reshape(b, SSM_HEADS, SSM_HEAD_DIM, D_STATE)


def compress(rows, w1, w2, pe):
    b, L = rows.shape[:2]
    nf = L // CMP_STRIDE
    ch = rows[:, :nf * CMP_STRIDE].reshape(b, nf, CMP_STRIDE, NSA_KV_GROUPS, HEAD_DIM)
    first = jnp.einsum('bnsgd,sde->bnge', ch, w1[:CMP_STRIDE])
    second = jnp.einsum('bnsgd,sde->bnge', ch, w1[CMP_STRIDE:])
    pe_bias = jnp.einsum('ld,lde->e', pe, w1)
    hid = jax.nn.gelu(first[:, :-1] + second[:, 1:] + pe_bias)
    return jnp.einsum('bnge,ed->bngd', hid, w2)


def cmp_to_sel(n_cb, n_sb):
    i = jnp.arange(n_cb)[:, None] * CMP_STRIDE
    j = jnp.arange(n_sb)[None, :] * SEL_BLOCK
    return ((i < j + SEL_BLOCK) & (i + CMP_LEN > j)).astype(F32)


def cmp_attend(q, kc, vc, q_pos):
    n = kc.shape[1]
    s = jnp.einsum('btgrd,bngd->bgrtn', q, kc, preferred_element_type=F32) * ATTN_SCALE
    end = jnp.arange(n) * CMP_STRIDE + CMP_LEN - 1
    p = masked_softmax(s, end[None, :] <= q_pos[:, None])
    o = jnp.einsum('bgrtn,bngd->btgrd', p.astype(vc.dtype), vc)
    return o, p


def select_blocks(imp, q_pos):
    n_sb = imp.shape[-1]
    j = jnp.arange(n_sb)[None, :]
    cur = (q_pos // SEL_BLOCK)[:, None]
    valid = (j * SEL_BLOCK <= q_pos[:, None])[:, None, :]
    forced = ((j == 0) | (j == cur) | (j == cur - 1))[:, None, :]
    score = jnp.where(valid, imp + jnp.where(forced, FORCE_BONUS, 0.0), NEG_INF)
    top, idx = lax.top_k(score, min(N_SEL, n_sb))
    return idx, top > 0.5 * NEG_INF


def slc_attend(q, kb, vb, idx, sel_ok, q_pos):
    kpos = idx[..., None] * SEL_BLOCK + jnp.arange(SEL_BLOCK)
    mask = sel_ok[..., None] & (kpos <= q_pos[None, :, None, None, None])
    s = jnp.einsum('btgrd,btgkld->btgrkl', q, kb, preferred_element_type=F32) * ATTN_SCALE
    b, T, G, R, K, L = s.shape
    p = masked_softmax(s.reshape(b, T, G, R, K * L), mask.reshape(b, T, G, 1, K * L))
    return jnp.einsum('btgrkl,btgkld->btgrd', p.reshape(s.shape).astype(vb.dtype), vb)


def slc_prompt(q, k, v, idx, sel_ok, q_pos):
    b, T = q.shape[:2]
    nsb = T // SEL_BLOCK
    kblk = k.reshape(b, nsb, SEL_BLOCK, NSA_KV_GROUPS, HEAD_DIM)
    vblk = v.reshape(b, nsb, SEL_BLOCK, NSA_KV_GROUPS, HEAD_DIM)
    bi = jnp.arange(b)[:, None, None, None]
    gi = jnp.arange(NSA_KV_GROUPS)[None, None, :, None]
    nq = T // SLC_QCHUNK

    def chunked(a):
        return jnp.moveaxis(a.reshape((b, nq, SLC_QCHUNK) + a.shape[2:]), 1, 0)

    def body(args):
        qc, ic, okc, pc = args
        kb = kblk[bi, ic, :, gi, :]
        vb = vblk[bi, ic, :, gi, :]
        return slc_attend(qc, kb, vb, ic, okc, pc)

    out = lax.map(body, (chunked(q), chunked(idx), chunked(sel_ok), q_pos.reshape(nq, SLC_QCHUNK)))
    return jnp.moveaxis(out, 0, 1).reshape(q.shape)


def gather_selected(pool, new_rows, idx, page_table, past):
    b, S = new_rows.shape[:2]
    bpp = PAGE_SIZE // SEL_BLOCK
    n_past_blk = past // SEL_BLOCK
    n_new_blk = -(-S // SEL_BLOCK)
    bi = jnp.arange(b)[:, None, None, None]
    gi = jnp.arange(NSA_KV_GROUPS)[None, None, :, None]
    jp = jnp.minimum(idx, n_past_blk - 1)
    phys = page_table[bi, jp // bpp]
    pool_blk = pool.reshape(pool.shape[0], bpp, SEL_BLOCK, NSA_KV_GROUPS, HEAD_DIM)
    from_past = pool_blk[phys, jp % bpp, :, gi, :]
    new_blk = jnp.pad(new_rows, ((0, 0), (0, n_new_blk * SEL_BLOCK - S), (0, 0), (0, 0)))
    new_blk = new_blk.reshape(b, n_new_blk, SEL_BLOCK, NSA_KV_GROUPS, HEAD_DIM)
    jn = jnp.clip(idx - n_past_blk, 0, n_new_blk - 1)
    from_new = new_blk[bi, jn, :, gi, :]
    return jnp.where((idx < n_past_blk)[..., None, None], from_past, from_new.astype(from_past.dtype))


def window_attend(q, k, v, q_pos, k_pos):
    s = jnp.einsum('btgrd,bsgd->bgrts', q, k, preferred_element_type=F32) * ATTN_SCALE
    diff = q_pos[:, None] - k_pos[None, :]
    mask = (diff >= 0) & (diff < WINDOW) & (k_pos[None, :] >= 0)
    p = masked_softmax(s, mask)
    return jnp.einsum('bgrts,bsgd->btgrd', p.astype(v.dtype), v)


def win_prompt(q, k, v):
    b, T = q.shape[:2]
    nq = T // WIN_QBLOCK
    span = WIN_QBLOCK + WINDOW
    kp = jnp.pad(k, ((0, 0), (WINDOW, 0), (0, 0), (0, 0)))
    vp = jnp.pad(v, ((0, 0), (WINDOW, 0), (0, 0), (0, 0)))
    qb = jnp.moveaxis(q.reshape((b, nq, WIN_QBLOCK) + q.shape[2:]), 1, 0)

    def body(args):
        i, qc = args
        start = i * WIN_QBLOCK
        kc = lax.dynamic_slice_in_dim(kp, start, span, axis=1)
        vc = lax.dynamic_slice_in_dim(vp, start, span, axis=1)
        q_pos = start + jnp.arange(WIN_QBLOCK)
        k_pos = start - WINDOW + jnp.arange(span)
        return window_attend(qc, kc, vc, q_pos, k_pos)

    out = lax.map(body, (jnp.arange(nq), qb))
    return jnp.moveaxis(out, 0, 1).reshape(q.shape)


def combine_nsa(gate, o_cmp, o_slc, o_win, dtype):
    b, T = o_cmp.shape[:2]
    o = gate[:, :, 0] * o_cmp + gate[:, :, 1] * o_slc + gate[:, :, 2] * o_win
    return o.reshape(b, T, NSA_HEADS * HEAD_DIM).astype(dtype)


def nsa_prompt(q, kv, gate, phi):
    phi1_k, phi2_k, pe_k, phi1_v, phi2_v, pe_v = phi
    b, T = q.shape[:2]
    pos = jnp.arange(T, dtype=jnp.int32)
    q = rotary(q, pos)
    k_cmp, k_slc, k_win = rotary(kv[:, :, 0], pos), rotary(kv[:, :, 2], pos), rotary(kv[:, :, 4], pos)
    v_cmp, v_slc, v_win = kv[:, :, 1], kv[:, :, 3], kv[:, :, 5]
    kc = compress(k_cmp, phi1_k, phi2_k, pe_k)
    vc = compress(v_cmp, phi1_v, phi2_v, pe_v)
    o_cmp, p_cmp = cmp_attend(q, kc, vc, pos)
    imp = jnp.einsum('bgrtn,nj->btgj', p_cmp, cmp_to_sel(kc.shape[1], -(-T // SEL_BLOCK)))
    idx, sel_ok = select_blocks(imp, pos)
    o_slc = slc_prompt(q, k_slc, v_slc, idx, sel_ok, pos)
    o_win = win_prompt(q, k_win, v_win)
    wb = min(WINDOW, T)
    o = combine_nsa(gate, o_cmp, o_slc, o_win, q.dtype)
    return o, (k_cmp, v_cmp, k_slc, v_slc, k_win[:, T - wb:], v_win[:, T - wb:])


def nsa_sample(q, kv, gate, phi, ck_cmp, cv_cmp, ck_slc, cv_slc, ck_win, cv_win, page_table):
    phi1_k, phi2_k, pe_k, phi1_v, phi2_v, pe_v = phi
    b, S = q.shape[:2]
    past = page_table.shape[1] * PAGE_SIZE
    pos = past + jnp.arange(S, dtype=jnp.int32)
    q = rotary(q, pos)
    k_cmp, k_slc, k_win = rotary(kv[:, :, 0], pos), rotary(kv[:, :, 2], pos), rotary(kv[:, :, 4], pos)
    v_cmp, v_slc, v_win = kv[:, :, 1], kv[:, :, 3], kv[:, :, 5]

    def gather_past(pool):
        return pool[page_table].reshape(b, past, NSA_KV_GROUPS, HEAD_DIM).astype(k_cmp.dtype)

    kc = compress(jnp.concatenate([gather_past(ck_cmp), k_cmp], axis=1), phi1_k, phi2_k, pe_k)
    vc = compress(jnp.concatenate([gather_past(cv_cmp), v_cmp], axis=1), phi1_v, phi2_v, pe_v)
    o_cmp, p_cmp = cmp_attend(q, kc, vc, pos)
    imp = jnp.einsum('bgrtn,nj->btgj', p_cmp, cmp_to_sel(kc.shape[1], -(-(past + S) // SEL_BLOCK)))
    idx, sel_ok = select_blocks(imp, pos)
    kb = gather_selected(ck_slc, k_slc, idx, page_table, past)
    vb = gather_selected(cv_slc, v_slc, idx, page_table, past)
    o_slc = slc_attend(q, kb, vb, idx, sel_ok, pos)
    wb = ck_win.shape[1]
    k_all = jnp.concatenate([ck_win.astype(k_win.dtype), k_win], axis=1)
    v_all = jnp.concatenate([cv_win.astype(v_win.dtype), v_win], axis=1)
    k_pos = past - wb + jnp.arange(wb + S)
    o_win = window_attend(q, k_all, v_all, pos, k_pos)
    o = combine_nsa(gate, o_cmp, o_slc, o_win, q.dtype)
    return o, (k_cmp, v_cmp, k_slc, v_slc, k_all[:, -wb:], v_all[:, -wb:])


def token_mixer(h, conv0, ssm0, nsa_fn, norm_w, w_in, conv_w, conv_b, dt_bias, a_log, d_skip,
                ssm_norm_w, w_br_ssm, w_br_nsa, w_out):
    b, T, _ = h.shape
    u = rmsnorm(h, norm_w)
    z, xbc, dt_raw, q, kv, g_nsa, g_merge = split_in(mm(u, w_in))
    y_ssm, conv_new, ssm_new = ssm_branch(z, xbc, dt_raw, conv0, ssm0, conv_w, conv_b,
                                          dt_bias, a_log, d_skip, ssm_norm_w)
    q = q.reshape(b, T, NSA_KV_GROUPS, NSA_REP, HEAD_DIM)
    kv = kv.reshape(b, T, 6, NSA_KV_GROUPS, HEAD_DIM)
    gate = jax.nn.sigmoid(g_nsa.astype(F32)).reshape(b, T, 3, NSA_KV_GROUPS, NSA_REP, 1)
    o_nsa, nsa_state = nsa_fn(q, kv, gate)
    gm = jax.nn.sigmoid(g_merge.astype(F32)).reshape(b, T, 2, D_MODEL)
    merged = gm[:, :, 0] * mm(y_ssm, w_br_ssm) + gm[:, :, 1] * mm(o_nsa, w_br_nsa)
    out = mm(merged, w_out)
    return out, nsa_state + (ssm_new.astype(h.dtype), conv_new)


def swiglu(u, wg, wu, wd):
    return mm(jax.nn.silu(mm(u, wg)) * mm(u, wu), wd)


def moe(u, w_router, wg, wu, wd):
    logits = (u @ w_router).astype(F32)
    top_l, top_i = lax.top_k(logits, TOP_K)
    top_w = jax.nn.softmax(top_l, axis=-1)
    gates = jnp.sum(jax.nn.one_hot(top_i, N_EXPERTS, dtype=F32) * top_w[..., None], axis=-2)
    y = jnp.zeros(u.shape[:-1] + (D_MODEL,), F32)
    for e in range(N_EXPERTS):
        y = y + gates[..., e:e + 1] * swiglu(u, wg[e], wu[e], wd[e])
    return y


def channel_mixer(h, l, norm_w, w_gate, w_up, w_down, w_router, w_gate_e, w_up_e, w_down_e):
    u = rmsnorm(h, norm_w)
    i = l // 2
    if l % 2 == 0:
        return swiglu(u, w_gate[i], w_up[i], w_down[i])
    return moe(u, w_router[i], w_gate_e[i], w_up_e[i], w_down_e[i])


def kernel(x_prompt, x_sample, cache_k_cmp, cache_v_cmp, cache_k_slc, cache_v_slc, cache_k_win, cache_v_win, state_ssm, state_conv, page_table, norm_mix, w_in, conv_w, conv_b, dt_bias, a_log, d_skip, ssm_norm, phi1_k, phi2_k, pe_k, phi1_v, phi2_v, pe_v, w_br_ssm, w_br_nsa, w_out, norm_ffn, w_gate, w_up, w_down, w_router, w_gate_e, w_up_e, w_down_e, norm_final):
    hp, hs = x_prompt, x_sample
    bp = x_prompt.shape[0]
    p_states = [[] for _ in range(8)]
    s_states = [[] for _ in range(8)]
    for l in range(DEPTH):
        mw = (norm_mix[l], w_in[l], conv_w[l], conv_b[l], dt_bias[l], a_log[l], d_skip[l],
              ssm_norm[l], w_br_ssm[l], w_br_nsa[l], w_out[l])
        phi = (phi1_k[l], phi2_k[l], pe_k[l], phi1_v[l], phi2_v[l], pe_v[l])
        ffn = (w_gate, w_up, w_down, w_router, w_gate_e, w_up_e, w_down_e)
        conv0 = jnp.zeros((bp, CONV_W - 1, CONV_DIM), hp.dtype)
        ssm0 = jnp.zeros((bp, SSM_HEADS, SSM_HEAD_DIM, D_STATE), F32)
        out, st = token_mixer(hp, conv0, ssm0, functools.partial(nsa_prompt, phi=phi), *mw)
        hp = hp + out
        hp = hp + channel_mixer(hp, l, norm_ffn[l], *ffn)
        for i in range(8):
            p_states[i].append(st[i])
        nsa_fn = functools.partial(nsa_sample, phi=phi, ck_cmp=cache_k_cmp[l], cv_cmp=cache_v_cmp[l],
                                   ck_slc=cache_k_slc[l], cv_slc=cache_v_slc[l],
                                   ck_win=cache_k_win[l], cv_win=cache_v_win[l], page_table=page_table)
        out, st = token_mixer(hs, state_conv[l], state_ssm[l], nsa_fn, *mw)
        hs = hs + out
        hs = hs + channel_mixer(hs, l, norm_ffn[l], *ffn)
        for i in range(8):
            s_states[i].append(st[i])
    y_prompt = rmsnorm(hp, norm_final)
    y_sample = rmsnorm(hs, norm_final)
    p_out = [jnp.stack(a) for a in p_states]
    s_out = [jnp.stack(a) for a in s_states]
    return (y_prompt, y_sample, *p_out, *s_out)
```

```python
import functools
import math

import jax
import jax.numpy as jnp
import numpy as np
from jax import lax
from jax.experimental import pallas as pl
from jax.experimental.pallas import tpu as pltpu

F32 = jnp.float32
BF16 = jnp.bfloat16

D_MODEL = 2048
DEPTH = 2
PAGE_SIZE = 128
D_INNER = 4096
SSM_HEAD_DIM = 64
SSM_HEADS = 64
SSM_GROUPS = 8
SSM_REP = 8
D_STATE = 128
CONV_W = 4
CONV_DIM = D_INNER + 2 * SSM_GROUPS * D_STATE
SSD_CHUNK = 128
NSA_HEADS = 16
NSA_KV_GROUPS = 4
NSA_REP = 4
HEAD_DIM = 128
KV_WIDTH = NSA_KV_GROUPS * HEAD_DIM
ROT_DIM = HEAD_DIM // 4
ROPE_THETA = 500000.0
ATTN_SCALE = HEAD_DIM ** -0.5
CMP_STRIDE = 16
CMP_LEN = 32
SEL_BLOCK = 64
SEL_SHIFT = 6
N_SEL = 16
WINDOW = 512
WIN_QBLOCK = 128
SLC_QCHUNK = 16
FORCE_BONUS = 1.0e4
NEG_INF = -1.0e30
N_EXPERTS = 8
TOP_K = 2
RMS_EPS = 1e-6
IN_SIZES = (D_INNER, CONV_DIM, SSM_HEADS, NSA_HEADS * HEAD_DIM, 6 * KV_WIDTH, 3 * NSA_HEADS, 2 * D_MODEL)

V7X_VMEM_LIMIT_BYTES = 56 * 1024 * 1024
LANE = 128


def _mm_kernel(x_ref, w_ref, o_ref, acc_ref):
    k = pl.program_id(2)

    @pl.when(k == 0)
    def _():
        acc_ref[...] = jnp.zeros_like(acc_ref)

    acc_ref[...] += jnp.dot(x_ref[...].astype(BF16), w_ref[...].astype(BF16),
                            preferred_element_type=F32)

    @pl.when(k == pl.num_programs(2) - 1)
    def _():
        o_ref[...] = acc_ref[...]


def _pick_tk(K, cap=2048):
    best = None
    for t in range(LANE, min(K, cap) + 1, LANE):
        if K % t == 0:
            best = t
    return best if best is not None else K


def pmm(x, w, tm=1024, tn=512):
    M, K = x.shape
    K2, N = w.shape
    assert K == K2
    tm = min(tm, M)
    tn = min(tn, N)
    tk = _pick_tk(K)
    assert M % tm == 0
    grid = (M // tm, pl.cdiv(N, tn), K // tk)
    return pl.pallas_call(
        _mm_kernel,
        grid=grid,
        in_specs=[pl.BlockSpec((tm, tk), lambda i, j, k: (i, k)),
                  pl.BlockSpec((tk, tn), lambda i, j, k: (k, j))],
        out_specs=pl.BlockSpec((tm, tn), lambda i, j, k: (i, j)),
        out_shape=jax.ShapeDtypeStruct((M, N), F32),
        scratch_shapes=[pltpu.VMEM((tm, tn), F32)],
        compiler_params=pltpu.CompilerParams(
            dimension_semantics=("parallel", "parallel", "arbitrary"),
            vmem_limit_bytes=V7X_VMEM_LIMIT_BYTES),
        name="pmm",
    )(x, w)


def mm(x, w):
    lead = x.shape[:-1]
    return pmm(x.reshape(-1, x.shape[-1]), w).reshape(lead + (w.shape[-1],))


NSA_TQ = 128
NSA_TK = 256
CB_PAD = 128


def _nt_dot(a, b):
    return lax.dot_general(a, b, (((1,), (1,)), ((), ())), preferred_element_type=F32)


def _split3_dot(x, w):
    hi = x.astype(BF16)
    r1 = x - hi.astype(F32)
    mid = r1.astype(BF16)
    lo = (r1 - mid.astype(F32)).astype(BF16)
    return (jnp.dot(hi, w, preferred_element_type=F32) + jnp.dot(mid, w, preferred_element_type=F32)
            + jnp.dot(lo, w, preferred_element_type=F32))


def _flash_tiles(q4, k_ref, v_ref, lo, hi, mask_fn):
    rows = q4.shape[0]

    def body(jt, carry):
        m, l, acc = carry
        k0 = pl.multiple_of(jt * NSA_TK, NSA_TK)
        kt = k_ref[0, pl.ds(k0, NSA_TK), :].astype(BF16)
        vt = v_ref[0, pl.ds(k0, NSA_TK), :].astype(BF16)
        s = _nt_dot(q4, kt) * ATTN_SCALE
        mk = mask_fn(k0)
        mk4 = jnp.concatenate([mk] * NSA_REP, axis=0) > 0.5
        sm = jnp.where(mk4, s, NEG_INF)
        m_new = jnp.maximum(m, jnp.max(sm, axis=-1, keepdims=True))
        alpha = jnp.exp(m - m_new)
        e = jnp.where(mk4, jnp.exp(sm - m_new), 0.0)
        l = alpha * l + jnp.sum(e, axis=-1, keepdims=True)
        acc = alpha * acc + jnp.dot(e.astype(BF16), vt, preferred_element_type=F32)
        return m_new, l, acc

    init = (jnp.full((rows, 1), NEG_INF, F32), jnp.zeros((rows, 1), F32), jnp.zeros((rows, HEAD_DIM), F32))
    _, l, acc = lax.fori_loop(lo, hi, body, init)
    return acc / jnp.where(l > 0.0, l, 1.0)


def _nsa_prompt_kernel(q_ref, kc_ref, vc_ref, ks_ref, vs_ref, kw_ref, vw_ref, gate_ref, c2s_ref, o_ref,
                       *, n_cb, n_sb):
    tq = NSA_TQ
    q0 = pl.program_id(2) * tq
    qb = q_ref[0]
    q4 = jnp.concatenate([qb[:, r * HEAD_DIM:(r + 1) * HEAD_DIM] for r in range(NSA_REP)], axis=0)

    s = _nt_dot(q4, kc_ref[0, 0]) * ATTN_SCALE
    row = lax.broadcasted_iota(jnp.int32, s.shape, 0)
    tpos = q0 + (row & (tq - 1))
    n = lax.broadcasted_iota(jnp.int32, s.shape, 1)
    cmask = (n * CMP_STRIDE + (CMP_LEN - 1) <= tpos) & (n < n_cb)
    sm = jnp.where(cmask, s, NEG_INF)
    m_c = jnp.max(sm, axis=-1, keepdims=True)
    e = jnp.where(cmask, jnp.exp(sm - m_c), 0.0)
    l_c = jnp.sum(e, axis=-1, keepdims=True)
    p = e / jnp.where(l_c > 0.0, l_c, 1.0)
    o_c = jnp.dot(p.astype(BF16), vc_ref[0, 0], preferred_element_type=F32)

    psum = p[0:tq] + p[tq:2 * tq] + p[2 * tq:3 * tq] + p[3 * tq:4 * tq]
    imp_t = _split3_dot(psum, c2s_ref[...]).T
    imp_t = imp_t[:n_sb]
    j = lax.broadcasted_iota(jnp.int32, imp_t.shape, 0)
    t = q0 + lax.broadcasted_iota(jnp.int32, imp_t.shape, 1)
    cur = jnp.right_shift(t, SEL_SHIFT)
    valid = j * SEL_BLOCK <= t
    forced = (j == 0) | (j == cur) | (j == cur - 1)
    score = jnp.where(valid, imp_t + jnp.where(forced, FORCE_BONUS, 0.0), NEG_INF)
    rank = jnp.zeros(score.shape, jnp.int32)
    for i in range(n_sb):
        si = score[i:i + 1, :]
        beats = (si > score) | ((si == score) & (i < j))
        rank = rank + jnp.where(beats, 1, 0)
    sel_t = jnp.where((rank < N_SEL) & (score > 0.5 * NEG_INF), 1.0, 0.0)
    sel_t = jnp.concatenate([sel_t, jnp.zeros((CB_PAD - n_sb, tq), F32)], axis=0)
    sel = sel_t.T.astype(BF16)

    def slc_mask(k0):
        jb = lax.broadcasted_iota(jnp.int32, (CB_PAD, NSA_TK), 0)
        kc = k0 + lax.broadcasted_iota(jnp.int32, (CB_PAD, NSA_TK), 1)
        expand = jnp.where(jb == jnp.right_shift(kc, SEL_SHIFT), 1.0, 0.0).astype(BF16)
        selx = jnp.dot(sel, expand, preferred_element_type=F32)
        qpos = q0 + lax.broadcasted_iota(jnp.int32, (tq, NSA_TK), 0)
        kpos = k0 + lax.broadcasted_iota(jnp.int32, (tq, NSA_TK), 1)
        return jnp.where((selx > 0.5) & (kpos <= qpos), 1.0, 0.0)

    def win_mask(k0):
        qpos = q0 + lax.broadcasted_iota(jnp.int32, (tq, NSA_TK), 0)
        kpos = k0 + lax.broadcasted_iota(jnp.int32, (tq, NSA_TK), 1)
        d = qpos - kpos
        return jnp.where((d >= 0) & (d < WINDOW), 1.0, 0.0)

    hi = (q0 + tq - 1) // NSA_TK + 1
    o_s = _flash_tiles(q4, ks_ref, vs_ref, 0, hi, slc_mask)
    o_w = _flash_tiles(q4, kw_ref, vw_ref, jnp.maximum(q0 - (WINDOW - 1), 0) // NSA_TK, hi, win_mask)

    g = jax.nn.sigmoid(gate_ref[0, 0])
    for r in range(NSA_REP):
        rs = slice(r * tq, (r + 1) * tq)
        o = (g[:, r:r + 1] * o_c[rs] + g[:, NSA_REP + r:NSA_REP + r + 1] * o_s[rs]
             + g[:, 2 * NSA_REP + r:2 * NSA_REP + r + 1] * o_w[rs])
        o_ref[0, :, r * HEAD_DIM:(r + 1) * HEAD_DIM] = o.astype(o_ref.dtype)


def nsa_prompt_attention(q, kc, vc, k_slc, v_slc, k_win, v_win, g_nsa):
    b, T, _ = q.shape
    n_cb = kc.shape[1]
    n_sb = T // SEL_BLOCK
    assert T % NSA_TK == 0 and n_cb <= CB_PAD and n_sb <= CB_PAD and n_sb % 8 == 0
    G = NSA_KV_GROUPS

    def pack_c(x):
        x = jnp.transpose(x, (0, 2, 1, 3))
        return jnp.pad(x, ((0, 0), (0, 0), (0, CB_PAD - n_cb), (0, 0))).astype(BF16)

    gates = jnp.transpose(g_nsa.reshape(b, T, 3, G, NSA_REP), (0, 3, 1, 2, 4)).reshape(b, G, T, 3 * NSA_REP)
    i = np.arange(CB_PAD)[:, None] * CMP_STRIDE
    jj = np.arange(CB_PAD)[None, :] * SEL_BLOCK
    c2s = ((i < jj + SEL_BLOCK) & (i + CMP_LEN > jj) & (np.arange(CB_PAD)[:, None] < n_cb)
           & (np.arange(CB_PAD)[None, :] < n_sb))
    c2s = jnp.asarray(c2s, BF16)
    flat = lambda x: x.reshape(b, T, G * HEAD_DIM)
    kv_spec = pl.BlockSpec((1, T, HEAD_DIM), lambda bi, g, qi: (bi, 0, g))
    c_spec = pl.BlockSpec((1, 1, CB_PAD, HEAD_DIM), lambda bi, g, qi: (bi, g, 0, 0))
    return pl.pallas_call(
        functools.partial(_nsa_prompt_kernel, n_cb=n_cb, n_sb=n_sb),
        grid=(b, G, T // NSA_TQ),
        in_specs=[pl.BlockSpec((1, NSA_TQ, NSA_REP * HEAD_DIM), lambda bi, g, qi: (bi, qi, g)),
                  c_spec, c_spec, kv_spec, kv_spec, kv_spec, kv_spec,
                  pl.BlockSpec((1, 1, NSA_TQ, 3 * NSA_REP), lambda bi, g, qi: (bi, g, qi, 0)),
                  pl.BlockSpec((CB_PAD, CB_PAD), lambda bi, g, qi: (0, 0))],
        out_specs=pl.BlockSpec((1, NSA_TQ, NSA_REP * HEAD_DIM), lambda bi, g, qi: (bi, qi, g)),
        out_shape=jax.ShapeDtypeStruct((b, T, NSA_HEADS * HEAD_DIM), BF16),
        compiler_params=pltpu.CompilerParams(
            dimension_semantics=("parallel", "parallel", "arbitrary"),
            vmem_limit_bytes=V7X_VMEM_LIMIT_BYTES),
        name="nsa_prompt",
    )(q, pack_c(kc), pack_c(vc), flat(k_slc), flat(v_slc), flat(k_win), flat(v_win), gates, c2s)


MOE_TM = 1024
MOE_TN = 512
MOE_TK = 512


def _moe_up_kernel(te_ref, nu_ref, x_ref, wg_ref, wu_ref, h_ref):
    @pl.when(pl.program_id(0) < nu_ref[0])
    def _():
        x = x_ref[...]
        g = jnp.dot(x, wg_ref[0].astype(BF16), preferred_element_type=F32)
        u = jnp.dot(x, wu_ref[0].astype(BF16), preferred_element_type=F32)
        h_ref[...] = (g * jax.nn.sigmoid(g) * u).astype(h_ref.dtype)


def _moe_down_kernel(te_ref, nu_ref, h_ref, wd_ref, gw_ref, y_ref):
    k = pl.program_id(1)

    @pl.when(pl.program_id(0) < nu_ref[0])
    def _():
        part = jnp.dot(h_ref[...], wd_ref[0].astype(BF16), preferred_element_type=F32)

        @pl.when(k == 0)
        def _():
            y_ref[...] = part

        @pl.when(k > 0)
        def _():
            y_ref[...] += part

        @pl.when(k == pl.num_programs(1) - 1)
        def _():
            y_ref[...] = y_ref[...] * gw_ref[...]


def moe_grouped(x_sorted, row_w, tile_expert, n_used, wg, wu, wd, tm):
    P, D = x_sorted.shape
    E, _, FF = wg.shape
    tn = min(MOE_TN, FF)
    tk = min(MOE_TK, FF)
    assert P % tm == 0 and FF % tn == 0 and FF % tk == 0
    nt, nj, nk = P // tm, FF // tn, FF // tk

    def row(i, nu):
        return jnp.minimum(i, nu[0] - 1)

    def col(i, j, nu, last):
        return jnp.where(i < nu[0], j, last)

    h = pl.pallas_call(
        _moe_up_kernel,
        grid_spec=pltpu.PrefetchScalarGridSpec(
            num_scalar_prefetch=2, grid=(nt, nj),
            in_specs=[pl.BlockSpec((tm, D), lambda i, j, te, nu: (row(i, nu), 0)),
                      pl.BlockSpec((1, D, tn), lambda i, j, te, nu: (te[row(i, nu)], 0, col(i, j, nu, nj - 1))),
                      pl.BlockSpec((1, D, tn), lambda i, j, te, nu: (te[row(i, nu)], 0, col(i, j, nu, nj - 1)))],
            out_specs=pl.BlockSpec((tm, tn), lambda i, j, te, nu: (row(i, nu), col(i, j, nu, nj - 1)))),
        out_shape=jax.ShapeDtypeStruct((P, FF), BF16),
        compiler_params=pltpu.CompilerParams(dimension_semantics=("arbitrary", "arbitrary"),
                                             vmem_limit_bytes=V7X_VMEM_LIMIT_BYTES),
        name="moe_up",
    )(tile_expert, n_used, x_sorted, wg, wu)
    return pl.pallas_call(
        _moe_down_kernel,
        grid_spec=pltpu.PrefetchScalarGridSpec(
            num_scalar_prefetch=2, grid=(nt, nk),
            in_specs=[pl.BlockSpec((tm, tk), lambda i, k, te, nu: (row(i, nu), col(i, k, nu, nk - 1))),
                      pl.BlockSpec((1, tk, D), lambda i, k, te, nu: (te[row(i, nu)], col(i, k, nu, nk - 1), 0)),
                      pl.BlockSpec((tm, 1), lambda i, k, te, nu: (row(i, nu), 0))],
            out_specs=pl.BlockSpec((tm, D), lambda i, k, te, nu: (row(i, nu), 0))),
        out_shape=jax.ShapeDtypeStruct((P, D), F32),
        compiler_params=pltpu.CompilerParams(dimension_semantics=("arbitrary", "arbitrary"),
                                             vmem_limit_bytes=V7X_VMEM_LIMIT_BYTES),
        name="moe_down",
    )(tile_expert, n_used, h, wd, row_w)


def moe_topk(u, w_router, wg, wu, wd, tm=MOE_TM):
    N, D = u.shape
    E = wg.shape[0]
    logits = jnp.dot(u, w_router, precision=lax.Precision.HIGHEST)
    top_l, top_i = lax.top_k(logits, TOP_K)
    top_w = jax.nn.softmax(top_l, axis=-1)
    flat_e = top_i.reshape(-1)
    onehot = (flat_e[:, None] == jnp.arange(E, dtype=flat_e.dtype)[None, :]).astype(jnp.int32)
    csum = jnp.cumsum(onehot, axis=0)
    rank = jnp.take_along_axis(csum, flat_e[:, None], axis=1)[:, 0] - 1
    tiles_per_e = (csum[-1] + tm - 1) // tm
    tile_end = jnp.cumsum(tiles_per_e)
    dest = (tile_end - tiles_per_e)[flat_e] * tm + rank
    nt = (TOP_K * N + E * (tm - 1)) // tm + 1
    P = nt * tm
    tile_expert = jnp.minimum(jnp.searchsorted(tile_end, jnp.arange(nt), side='right'), E - 1).astype(jnp.int32)
    n_used = tile_end[-1:].astype(jnp.int32)
    row_token = jnp.zeros((P,), jnp.int32).at[dest].set(jnp.arange(TOP_K * N, dtype=jnp.int32) // TOP_K)
    row_w = jnp.zeros((P, 1), F32).at[dest, 0].set(top_w.reshape(-1))
    x_sorted = u.astype(BF16)[row_token]
    y_sorted = moe_grouped(x_sorted, row_w, tile_expert, n_used, wg, wu, wd, tm)
    picked = y_sorted[dest].reshape(N, TOP_K, D)
    y = picked[:, 0]
    for k in range(1, TOP_K):
        y = y + picked[:, k]
    return y


def rmsnorm(x, w):
    xf = x.astype(F32)
    y = xf * lax.rsqrt(jnp.mean(xf * xf, axis=-1, keepdims=True) + RMS_EPS)
    return (y * w.astype(F32)).astype(x.dtype)


def masked_softmax(s, mask):
    p = jax.nn.softmax(jnp.where(mask, s, NEG_INF), axis=-1)
    return p * mask


def rotary(x, pos):
    half = ROT_DIM // 2
    inv = ROPE_THETA ** (-jnp.arange(half, dtype=F32) / half)
    ang = pos.astype(F32)[:, None] * inv[None, :]
    ang = ang.reshape((1, pos.shape[0]) + (1,) * (x.ndim - 3) + (half,))
    cos, sin = jnp.cos(ang), jnp.sin(ang)
    xr = x[..., :ROT_DIM].astype(F32)
    x1, x2 = xr[..., :half], xr[..., half:]
    rot = jnp.concatenate([x1 * cos - x2 * sin, x2 * cos + x1 * sin], axis=-1)
    return jnp.concatenate([rot.astype(x.dtype), x[..., ROT_DIM:]], axis=-1)


def split_in(proj):
    cuts = np.cumsum(IN_SIZES)[:-1].tolist()
    return jnp.split(proj, cuts, axis=-1)


def ssd_chunked(x, dt, a, bm, cm, h0):
    b, T = x.shape[:2]
    q = SSD_CHUNK if T >= SSD_CHUNK else T
    nc = -(-T // q)
    pad = nc * q - T

    def chunks(v):
        v = jnp.pad(v.astype(F32), [(0, 0), (0, pad)] + [(0, 0)] * (v.ndim - 2))
        return jnp.moveaxis(v.reshape((b, nc, q) + v.shape[2:]), 1, 0)

    causal = jnp.tril(jnp.ones((q, q), bool))[None, :, :, None, None]

    def step(h, inp):
        xc, dtc, bc, cc = inp
        acum = jnp.cumsum(dtc * a, axis=1)
        seg = acum[:, :, None] - acum[:, None, :]
        lmat = jnp.exp(jnp.where(causal, seg, -jnp.inf))
        cb = jnp.einsum('blgn,bsgn->blsg', cc, bc)
        y_diag = jnp.einsum('blsg,blsgr,bsgrp->blgrp', cb, lmat, dtc[..., None] * xc)
        y_off = jnp.einsum('blgn,bgrpn->blgrp', cc, h) * jnp.exp(acum)[..., None]
        decay = jnp.exp(acum[:, -1:] - acum) * dtc
        h_new = jnp.exp(acum[:, -1])[..., None, None] * h + jnp.einsum('bsgn,bsgr,bsgrp->bgrpn', bc, decay, xc)
        return h_new, y_diag + y_off

    h_fin, ys = lax.scan(step, h0.astype(F32), (chunks(x), chunks(dt), chunks(bm), chunks(cm)))
    y = jnp.moveaxis(ys, 0, 1).reshape((b, nc * q) + x.shape[2:])[:, :T]
    return y, h_fin


def ssm_branch(z, xbc, dt_raw, conv0, h0, conv_w, conv_b, dt_bias, a_log, d_skip, norm_w):
    b, T, _ = xbc.shape
    xpad = jnp.concatenate([conv0.astype(xbc.dtype), xbc], axis=1)
    acc = conv_b
    for k in range(CONV_W):
        acc = acc + xpad[:, k:k + T] * conv_w[k]
    xbc_c = jax.nn.silu(acc)
    new_conv = xpad[:, T:]
    xs, bm, cm = jnp.split(xbc_c, [D_INNER, D_INNER + SSM_GROUPS * D_STATE], axis=-1)
    xs = xs.reshape(b, T, SSM_GROUPS, SSM_REP, SSM_HEAD_DIM)
    bm = bm.reshape(b, T, SSM_GROUPS, D_STATE)
    cm = cm.reshape(b, T, SSM_GROUPS, D_STATE)
    dt = jax.nn.softplus(dt_raw.astype(F32) + dt_bias.astype(F32)).reshape(b, T, SSM_GROUPS, SSM_REP)
    a = -jnp.exp(a_log.astype(F32)).reshape(SSM_GROUPS, SSM_REP)
    h0 = h0.reshape(b, SSM_GROUPS, SSM_REP, SSM_HEAD_DIM, D_STATE)
    y, h_fin = ssd_chunked(xs, dt, a, bm, cm, h0)
    y = y + d_skip.astype(F32).reshape(SSM_GROUPS, SSM_REP)[..., None] * xs.astype(F32)
    y = y.reshape(b, T, D_INNER) * jax.nn.silu(z.astype(F32))
    yg = y.reshape(b, T, SSM_GROUPS, D_INNER // SSM_GROUPS)
    yg = yg * lax.rsqrt(jnp.mean(yg * yg, axis=-1, keepdims=True) + RMS_EPS)
    y = yg.reshape(b, T, D_INNER) * norm_w.astype(F32)
    return y.astype(z.dtype), new_conv, h_fin.reshape(b, SSM_HEADS, SSM_HEAD_DIM, D_STATE)


def compress(rows, w1, w2, pe):
    b, L = rows.shape[:2]
    nf = L // CMP_STRIDE
    ch = rows[:, :nf * CMP_STRIDE].reshape(b, nf, CMP_STRIDE, NSA_KV_GROUPS, HEAD_DIM)
    first = jnp.einsum('bnsgd,sde->bnge', ch, w1[:CMP_STRIDE])
    second = jnp.einsum('bnsgd,sde->bnge', ch, w1[CMP_STRIDE:])
    pe_bias = jnp.einsum('ld,lde->e', pe, w1)
    hid = jax.nn.gelu(first[:, :-1] + second[:, 1:] + pe_bias)
    return jnp.einsum('bnge,ed->bngd', hid, w2)


def cmp_to_sel(n_cb, n_sb):
    i = jnp.arange(n_cb)[:, None] * CMP_STRIDE
    j = jnp.arange(n_sb)[None, :] * SEL_BLOCK
    return ((i < j + SEL_BLOCK) & (i + CMP_LEN > j)).astype(F32)


def cmp_attend(q, kc, vc, q_pos):
    n = kc.shape[1]
    s = jnp.einsum('btgrd,bngd->bgrtn', q, kc, preferred_element_type=F32) * ATTN_SCALE
    end = jnp.arange(n) * CMP_STRIDE + CMP_LEN - 1
    p = masked_softmax(s, end[None, :] <= q_pos[:, None])
    o = jnp.einsum('bgrtn,bngd->btgrd', p.astype(vc.dtype), vc)
    return o, p


def select_blocks(imp, q_pos):
    n_sb = imp.shape[-1]
    j = jnp.arange(n_sb)[None, :]
    cur = (q_pos // SEL_BLOCK)[:, None]
    valid = (j * SEL_BLOCK <= q_pos[:, None])[:, None, :]
    forced = ((j == 0) | (j == cur) | (j == cur - 1))[:, None, :]
    score = jnp.where(valid, imp + jnp.where(forced, FORCE_BONUS, 0.0), NEG_INF)
    top, idx = lax.top_k(score, min(N_SEL, n_sb))
    return idx, top > 0.5 * NEG_INF


def slc_attend(q, kb, vb, idx, sel_ok, q_pos):
    kpos = idx[..., None] * SEL_BLOCK + jnp.arange(SEL_BLOCK)
    mask = sel_ok[..., None] & (kpos <= q_pos[None, :, None, None, None])
    s = jnp.einsum('btgrd,btgkld->btgrkl', q, kb, preferred_element_type=F32) * ATTN_SCALE
    b, T, G, R, K, L = s.shape
    p = masked_softmax(s.reshape(b, T, G, R, K * L), mask.reshape(b, T, G, 1, K * L))
    return jnp.einsum('btgrkl,btgkld->btgrd', p.reshape(s.shape).astype(vb.dtype), vb)


def slc_prompt(q, k, v, idx, sel_ok, q_pos):
    b, T = q.shape[:2]
    nsb = T // SEL_BLOCK
    kblk = k.reshape(b, nsb, SEL_BLOCK, NSA_KV_GROUPS, HEAD_DIM)
    vblk = v.reshape(b, nsb, SEL_BLOCK, NSA_KV_GROUPS, HEAD_DIM)
    bi = jnp.arange(b)[:, None, None, None]
    gi = jnp.arange(NSA_KV_GROUPS)[None, None, :, None]
    nq = T // SLC_QCHUNK

    def chunked(a):
        return jnp.moveaxis(a.reshape((b, nq, SLC_QCHUNK) + a.shape[2:]), 1, 0)

    def body(args):
        qc, ic, okc, pc = args
        kb = kblk[bi, ic, :, gi, :]
        vb = vblk[bi, ic, :, gi, :]
        return slc_attend(qc, kb, vb, ic, okc, pc)

    out = lax.map(body, (chunked(q), chunked(idx), chunked(sel_ok), q_pos.reshape(nq, SLC_QCHUNK)))
    return jnp.moveaxis(out, 0, 1).reshape(q.shape)


def gather_selected(pool, new_rows, idx, page_table, past):
    b, S = new_rows.shape[:2]
    bpp = PAGE_SIZE // SEL_BLOCK
    n_past_blk = past // SEL_BLOCK
    n_new_blk = -(-S // SEL_BLOCK)
    bi = jnp.arange(b)[:, None, None, None]
    gi = jnp.arange(NSA_KV_GROUPS)[None, None, :, None]
    jp = jnp.minimum(idx, n_past_blk - 1)
    phys = page_table[bi, jp // bpp]
    pool_blk = pool.reshape(pool.shape[0], bpp, SEL_BLOCK, NSA_KV_GROUPS, HEAD_DIM)
    from_past = pool_blk[phys, jp % bpp, :, gi, :]
    new_blk = jnp.pad(new_rows, ((0, 0), (0, n_new_blk * SEL_BLOCK - S), (0, 0), (0, 0)))
    new_blk = new_blk.reshape(b, n_new_blk, SEL_BLOCK, NSA_KV_GROUPS, HEAD_DIM)
    jn = jnp.clip(idx - n_past_blk, 0, n_new_blk - 1)
    from_new = new_blk[bi, jn, :, gi, :]
    return jnp.where((idx < n_past_blk)[..., None, None], from_past, from_new.astype(from_past.dtype))


def window_attend(q, k, v, q_pos, k_pos):
    s = jnp.einsum('btgrd,bsgd->bgrts', q, k, preferred_element_type=F32) * ATTN_SCALE
    diff = q_pos[:, None] - k_pos[None, :]
    mask = (diff >= 0) & (diff < WINDOW) & (k_pos[None, :] >= 0)
    p = masked_softmax(s, mask)
    return jnp.einsum('bgrts,bsgd->btgrd', p.astype(v.dtype), v)


def win_prompt(q, k, v):
    b, T = q.shape[:2]
    nq = T // WIN_QBLOCK
    span = WIN_QBLOCK + WINDOW
    kp = jnp.pad(k, ((0, 0), (WINDOW, 0), (0, 0), (0, 0)))
    vp = jnp.pad(v, ((0, 0), (WINDOW, 0), (0, 0), (0, 0)))
    qb = jnp.moveaxis(q.reshape((b, nq, WIN_QBLOCK) + q.shape[2:]), 1, 0)

    def body(args):
        i, qc = args
        start = i * WIN_QBLOCK
        kc = lax.dynamic_slice_in_dim(kp, start, span, axis=1)
        vc = lax.dynamic_slice_in_dim(vp, start, span, axis=1)
        q_pos = start + jnp.arange(WIN_QBLOCK)
        k_pos = start - WINDOW + jnp.arange(span)
        return window_attend(qc, kc, vc, q_pos, k_pos)

    out = lax.map(body, (jnp.arange(nq), qb))
    return jnp.moveaxis(out, 0, 1).reshape(q.shape)


def combine_nsa(gate, o_cmp, o_slc, o_win, dtype):
    b, T = o_cmp.shape[:2]
    o = gate[:, :, 0] * o_cmp + gate[:, :, 1] * o_slc + gate[:, :, 2] * o_win
    return o.reshape(b, T, NSA_HEADS * HEAD_DIM).astype(dtype)


def nsa_prompt(q, kv, g_nsa, phi):
    phi1_k, phi2_k, pe_k, phi1_v, phi2_v, pe_v = phi
    b, T = q.shape[:2]
    pos = jnp.arange(T, dtype=jnp.int32)
    q = rotary(q, pos)
    k_cmp, k_slc, k_win = rotary(kv[:, :, 0], pos), rotary(kv[:, :, 2], pos), rotary(kv[:, :, 4], pos)
    v_cmp, v_slc, v_win = kv[:, :, 1], kv[:, :, 3], kv[:, :, 5]
    kc = compress(k_cmp, phi1_k, phi2_k, pe_k)
    vc = compress(v_cmp, phi1_v, phi2_v, pe_v)
    o = nsa_prompt_attention(q.reshape(b, T, NSA_HEADS * HEAD_DIM).astype(BF16), kc, vc,
                             k_slc, v_slc, k_win, v_win, g_nsa)
    wb = min(WINDOW, T)
    return o, (k_cmp, v_cmp, k_slc, v_slc, k_win[:, T - wb:], v_win[:, T - wb:])


def nsa_sample(q, kv, g_nsa, phi, ck_cmp, cv_cmp, ck_slc, cv_slc, ck_win, cv_win, page_table):
    phi1_k, phi2_k, pe_k, phi1_v, phi2_v, pe_v = phi
    b, S = q.shape[:2]
    gate = jax.nn.sigmoid(g_nsa.astype(F32)).reshape(b, S, 3, NSA_KV_GROUPS, NSA_REP, 1)
    past = page_table.shape[1] * PAGE_SIZE
    pos = past + jnp.arange(S, dtype=jnp.int32)
    q = rotary(q, pos)
    k_cmp, k_slc, k_win = rotary(kv[:, :, 0], pos), rotary(kv[:, :, 2], pos), rotary(kv[:, :, 4], pos)
    v_cmp, v_slc, v_win = kv[:, :, 1], kv[:, :, 3], kv[:, :, 5]

    def gather_past(pool):
        return pool[page_table].reshape(b, past, NSA_KV_GROUPS, HEAD_DIM).astype(k_cmp.dtype)

    kc = compress(jnp.concatenate([gather_past(ck_cmp), k_cmp], axis=1), phi1_k, phi2_k, pe_k)
    vc = compress(jnp.concatenate([gather_past(cv_cmp), v_cmp], axis=1), phi1_v, phi2_v, pe_v)
    o_cmp, p_cmp = cmp_attend(q, kc, vc, pos)
    imp = jnp.einsum('bgrtn,nj->btgj', p_cmp, cmp_to_sel(kc.shape[1], -(-(past + S) // SEL_BLOCK)))
    idx, sel_ok = select_blocks(imp, pos)
    kb = gather_selected(ck_slc, k_slc, idx, page_table, past)
    vb = gather_selected(cv_slc, v_slc, idx, page_table, past)
    o_slc = slc_attend(q, kb, vb, idx, sel_ok, pos)
    wb = ck_win.shape[1]
    k_all = jnp.concatenate([ck_win.astype(k_win.dtype), k_win], axis=1)
    v_all = jnp.concatenate([cv_win.astype(v_win.dtype), v_win], axis=1)
    k_pos = past - wb + jnp.arange(wb + S)
    o_win = window_attend(q, k_all, v_all, pos, k_pos)
    o = combine_nsa(gate, o_cmp, o_slc, o_win, q.dtype)
    return o, (k_cmp, v_cmp, k_slc, v_slc, k_all[:, -wb:], v_all[:, -wb:])


def token_mixer(h, conv0, ssm0, nsa_fn, norm_w, w_in, conv_w, conv_b, dt_bias, a_log, d_skip,
                ssm_norm_w, w_br_ssm, w_br_nsa, w_out):
    b, T, _ = h.shape
    u = rmsnorm(h, norm_w)
    z, xbc, dt_raw, q, kv, g_nsa, g_merge = split_in(mm(u, w_in))
    y_ssm, conv_new, ssm_new = ssm_branch(z, xbc, dt_raw, conv0, ssm0, conv_w, conv_b,
                                          dt_bias, a_log, d_skip, ssm_norm_w)
    q = q.reshape(b, T, NSA_KV_GROUPS, NSA_REP, HEAD_DIM)
    kv = kv.reshape(b, T, 6, NSA_KV_GROUPS, HEAD_DIM)
    o_nsa, nsa_state = nsa_fn(q, kv, g_nsa)
    gm = jax.nn.sigmoid(g_merge.astype(F32)).reshape(b, T, 2, D_MODEL)
    merged = gm[:, :, 0] * mm(y_ssm, w_br_ssm) + gm[:, :, 1] * mm(o_nsa, w_br_nsa)
    out = mm(merged, w_out)
    return out, nsa_state + (ssm_new.astype(h.dtype), conv_new)


def swiglu(u, wg, wu, wd):
    return mm(jax.nn.silu(mm(u, wg)) * mm(u, wu), wd)


def channel_mixer(hp, hs, l, norm_w, w_gate, w_up, w_down, w_router, w_gate_e, w_up_e, w_down_e):
    up, us = rmsnorm(hp, norm_w), rmsnorm(hs, norm_w)
    i = l // 2
    if l % 2 == 0:
        return swiglu(up, w_gate[i], w_up[i], w_down[i]), swiglu(us, w_gate[i], w_up[i], w_down[i])
    n_p = up.shape[0] * up.shape[1]
    u_all = jnp.concatenate([up.reshape(n_p, D_MODEL), us.reshape(-1, D_MODEL)], axis=0)
    y = moe_topk(u_all, w_router[i], w_gate_e[i], w_up_e[i], w_down_e[i])
    return y[:n_p].reshape(up.shape), y[n_p:].reshape(us.shape)


def kernel(x_prompt, x_sample, cache_k_cmp, cache_v_cmp, cache_k_slc, cache_v_slc, cache_k_win, cache_v_win, state_ssm, state_conv, page_table, norm_mix, w_in, conv_w, conv_b, dt_bias, a_log, d_skip, ssm_norm, phi1_k, phi2_k, pe_k, phi1_v, phi2_v, pe_v, w_br_ssm, w_br_nsa, w_out, norm_ffn, w_gate, w_up, w_down, w_router, w_gate_e, w_up_e, w_down_e, norm_final):
    hp, hs = x_prompt, x_sample
    bp = x_prompt.shape[0]
    p_states = [[] for _ in range(8)]
    s_states = [[] for _ in range(8)]
    for l in range(DEPTH):
        mw = (norm_mix[l], w_in[l], conv_w[l], conv_b[l], dt_bias[l], a_log[l], d_skip[l],
              ssm_norm[l], w_br_ssm[l], w_br_nsa[l], w_out[l])
        phi = (phi1_k[l], phi2_k[l], pe_k[l], phi1_v[l], phi2_v[l], pe_v[l])
        ffn = (w_gate, w_up, w_down, w_router, w_gate_e, w_up_e, w_down_e)
        conv0 = jnp.zeros((bp, CONV_W - 1, CONV_DIM), hp.dtype)
        ssm0 = jnp.zeros((bp, SSM_HEADS, SSM_HEAD_DIM, D_STATE), F32)
        out, st = token_mixer(hp, conv0, ssm0, functools.partial(nsa_prompt, phi=phi), *mw)
        hp = hp + out
        for i in range(8):
            p_states[i].append(st[i])
        nsa_fn = functools.partial(nsa_sample, phi=phi, ck_cmp=cache_k_cmp[l], cv_cmp=cache_v_cmp[l],
                                   ck_slc=cache_k_slc[l], cv_slc=cache_v_slc[l],
                                   ck_win=cache_k_win[l], cv_win=cache_v_win[l], page_table=page_table)
        out, st = token_mixer(hs, state_conv[l], state_ssm[l], nsa_fn, *mw)
        hs = hs + out
        for i in range(8):
            s_states[i].append(st[i])
        fp, fs = channel_mixer(hp, hs, l, norm_ffn[l], *ffn)
        hp, hs = hp + fp, hs + fs
    y_prompt = rmsnorm(hp, norm_final)
    y_sample = rmsnorm(hs, norm_final)
    p_out = [jnp.stack(a) for a in p_states]
    s_out = [jnp.stack(a) for a in s_states]
    return (y_prompt, y_sample, *p_out, *s_out)
```

```python
import functools
import math

import jax
import jax.numpy as jnp
import numpy as np
from jax import lax
from jax.experimental import pallas as pl
from jax.experimental.pallas import tpu as pltpu

F32 = jnp.float32
BF16 = jnp.bfloat16

D_MODEL = 2048
DEPTH = 2
PAGE_SIZE = 128
D_INNER = 4096
SSM_HEAD_DIM = 64
SSM_HEADS = 64
SSM_GROUPS = 8
SSM_REP = 8
D_STATE = 128
CONV_W = 4
CONV_DIM = D_INNER + 2 * SSM_GROUPS * D_STATE
SSD_CHUNK = 128
NSA_HEADS = 16
NSA_KV_GROUPS = 4
NSA_REP = 4
HEAD_DIM = 128
KV_WIDTH = NSA_KV_GROUPS * HEAD_DIM
ROT_DIM = HEAD_DIM // 4
ROPE_THETA = 500000.0
ATTN_SCALE = HEAD_DIM ** -0.5
CMP_STRIDE = 16
CMP_LEN = 32
SEL_BLOCK = 64
SEL_SHIFT = 6
N_SEL = 16
WINDOW = 512
WIN_QBLOCK = 128
SLC_QCHUNK = 16
FORCE_BONUS = 1.0e4
NEG_INF = -1.0e30
N_EXPERTS = 8
TOP_K = 2
RMS_EPS = 1e-6
IN_SIZES = (D_INNER, CONV_DIM, SSM_HEADS, NSA_HEADS * HEAD_DIM, 6 * KV_WIDTH, 3 * NSA_HEADS, 2 * D_MODEL)

V7X_VMEM_LIMIT_BYTES = 56 * 1024 * 1024
LANE = 128


def _mm_kernel(x_ref, w_ref, o_ref, acc_ref):
    k = pl.program_id(2)

    @pl.when(k == 0)
    def _():
        acc_ref[...] = jnp.zeros_like(acc_ref)

    acc_ref[...] += jnp.dot(x_ref[...].astype(BF16), w_ref[...].astype(BF16),
                            preferred_element_type=F32)

    @pl.when(k == pl.num_programs(2) - 1)
    def _():
        o_ref[...] = acc_ref[...]


def _pick_tk(K, cap=2048):
    best = None
    for t in range(LANE, min(K, cap) + 1, LANE):
        if K % t == 0:
            best = t
    return best if best is not None else K


def pmm(x, w, tm=1024, tn=512):
    M, K = x.shape
    K2, N = w.shape
    assert K == K2
    tm = min(tm, M)
    tn = min(tn, N)
    tk = _pick_tk(K)
    assert M % tm == 0
    grid = (M // tm, pl.cdiv(N, tn), K // tk)
    return pl.pallas_call(
        _mm_kernel,
        grid=grid,
        in_specs=[pl.BlockSpec((tm, tk), lambda i, j, k: (i, k)),
                  pl.BlockSpec((tk, tn), lambda i, j, k: (k, j))],
        out_specs=pl.BlockSpec((tm, tn), lambda i, j, k: (i, j)),
        out_shape=jax.ShapeDtypeStruct((M, N), F32),
        scratch_shapes=[pltpu.VMEM((tm, tn), F32)],
        compiler_params=pltpu.CompilerParams(
            dimension_semantics=("parallel", "parallel", "arbitrary"),
            vmem_limit_bytes=V7X_VMEM_LIMIT_BYTES),
        name="pmm",
    )(x, w)


def mm(x, w):
    lead = x.shape[:-1]
    return pmm(x.reshape(-1, x.shape[-1]), w).reshape(lead + (w.shape[-1],))


def _merge_kernel(a_ref, wa_ref, b_ref, wb_ref, ga_ref, gb_ref, o_ref):
    ya = jnp.dot(a_ref[...].astype(BF16), wa_ref[...].astype(BF16), preferred_element_type=F32)
    yb = jnp.dot(b_ref[...].astype(BF16), wb_ref[...].astype(BF16), preferred_element_type=F32)
    o_ref[...] = (jax.nn.sigmoid(ga_ref[...]) * ya + jax.nn.sigmoid(gb_ref[...]) * yb).astype(o_ref.dtype)


def gated_merge(a, wa, b, wb, g_merge, tm=1024, tn=256):
    M, Ka = a.shape
    Kb = b.shape[1]
    N = wa.shape[1]
    tm, tn = min(tm, M), min(tn, N)
    assert M % tm == 0 and N % tn == 0 and g_merge.shape == (M, 2 * N)
    nj = N // tn
    return pl.pallas_call(
        _merge_kernel,
        grid=(M // tm, nj),
        in_specs=[pl.BlockSpec((tm, Ka), lambda i, j: (i, 0)),
                  pl.BlockSpec((Ka, tn), lambda i, j: (0, j)),
                  pl.BlockSpec((tm, Kb), lambda i, j: (i, 0)),
                  pl.BlockSpec((Kb, tn), lambda i, j: (0, j)),
                  pl.BlockSpec((tm, tn), lambda i, j: (i, j)),
                  pl.BlockSpec((tm, tn), lambda i, j: (i, j + nj))],
        out_specs=pl.BlockSpec((tm, tn), lambda i, j: (i, j)),
        out_shape=jax.ShapeDtypeStruct((M, N), BF16),
        compiler_params=pltpu.CompilerParams(dimension_semantics=("parallel", "parallel"),
                                             vmem_limit_bytes=V7X_VMEM_LIMIT_BYTES),
        name="gated_merge",
    )(a, wa, b, wb, g_merge, g_merge)


def _swiglu_up_kernel(x_ref, wg_ref, wu_ref, h_ref):
    x = x_ref[...].astype(BF16)
    g = jnp.dot(x, wg_ref[...].astype(BF16), preferred_element_type=F32)
    u = jnp.dot(x, wu_ref[...].astype(BF16), preferred_element_type=F32)
    h_ref[...] = (g * jax.nn.sigmoid(g) * u).astype(h_ref.dtype)


def swiglu_up(x, wg, wu, tm=1024, tn=512):
    M, D = x.shape
    FF = wg.shape[1]
    tm, tn = min(tm, M), min(tn, FF)
    assert M % tm == 0 and FF % tn == 0
    return pl.pallas_call(
        _swiglu_up_kernel,
        grid=(M // tm, FF // tn),
        in_specs=[pl.BlockSpec((tm, D), lambda i, j: (i, 0)),
                  pl.BlockSpec((D, tn), lambda i, j: (0, j)),
                  pl.BlockSpec((D, tn), lambda i, j: (0, j))],
        out_specs=pl.BlockSpec((tm, tn), lambda i, j: (i, j)),
        out_shape=jax.ShapeDtypeStruct((M, FF), BF16),
        compiler_params=pltpu.CompilerParams(dimension_semantics=("parallel", "parallel"),
                                             vmem_limit_bytes=V7X_VMEM_LIMIT_BYTES),
        name="swiglu_up",
    )(x, wg, wu)


ROPE_TR = 512


def _rope_kernel(x_ref, cos_ref, sin_ref, o_ref):
    cos, sin = cos_ref[...], sin_ref[...]
    lane = lax.broadcasted_iota(jnp.int32, cos.shape, 1)
    first = lane < ROT_DIM // 2
    for h in range(x_ref.shape[1] // HEAD_DIM):
        x = x_ref[:, h * HEAD_DIM:(h + 1) * HEAD_DIM].astype(F32)
        partner = jnp.where(first, pltpu.roll(x, HEAD_DIM - ROT_DIM // 2, 1), pltpu.roll(x, ROT_DIM // 2, 1))
        o_ref[:, h * HEAD_DIM:(h + 1) * HEAD_DIM] = (x * cos + partner * sin).astype(o_ref.dtype)


def rope_tables(pos):
    half = ROT_DIM // 2
    inv = ROPE_THETA ** (-jnp.arange(half, dtype=F32) / half)
    ang = pos.astype(F32)[:, None] * inv[None, :]
    cos, sin = jnp.cos(ang), jnp.sin(ang)
    n = pos.shape[0]
    cos_t = jnp.concatenate([cos, cos, jnp.ones((n, HEAD_DIM - ROT_DIM), F32)], axis=1)
    sin_t = jnp.concatenate([-sin, sin, jnp.zeros((n, HEAD_DIM - ROT_DIM), F32)], axis=1)
    return cos_t, sin_t


def rope_rows(x, cos_t, sin_t, out_dtype):
    M, C = x.shape
    T = cos_t.shape[0]
    tr = min(ROPE_TR, T)
    assert T % tr == 0 and M % T == 0 and C % HEAD_DIM == 0
    nt = T // tr
    return pl.pallas_call(
        _rope_kernel,
        grid=(M // tr,),
        in_specs=[pl.BlockSpec((tr, C), lambda i: (i, 0)),
                  pl.BlockSpec((tr, HEAD_DIM), lambda i: (i % nt, 0)),
                  pl.BlockSpec((tr, HEAD_DIM), lambda i: (i % nt, 0))],
        out_specs=pl.BlockSpec((tr, C), lambda i: (i, 0)),
        out_shape=jax.ShapeDtypeStruct((M, C), out_dtype),
        compiler_params=pltpu.CompilerParams(dimension_semantics=("parallel",),
                                             vmem_limit_bytes=V7X_VMEM_LIMIT_BYTES),
        name="rope",
    )(x, cos_t, sin_t)


CMP_PAGES = 8
CHUNKS_PER_PAGE = PAGE_SIZE // CMP_STRIDE


def _compress_kernel(pt_ref, *refs):
    page_refs, w_ref, o_ref = refs[:CMP_PAGES], refs[CMP_PAGES], refs[CMP_PAGES + 1]
    slabs = []
    for g in range(NSA_KV_GROUPS):
        for p_ref in page_refs:
            slabs.append(jnp.concatenate(
                [p_ref[0, 0, pl.ds(s * NSA_KV_GROUPS + g, CHUNKS_PER_PAGE, stride=CMP_STRIDE * NSA_KV_GROUPS), :]
                 for s in range(CMP_STRIDE)], axis=1))
    x = jnp.concatenate(slabs, axis=0).astype(BF16)
    y = jnp.dot(x, w_ref[...], preferred_element_type=F32)
    per_g = CMP_PAGES * CHUNKS_PER_PAGE
    for g in range(NSA_KV_GROUPS):
        o_ref[0, g] = y[g * per_g:(g + 1) * per_g]


def compress_first(pool, layer, page_flat, n_seq, w1):
    ppb = page_flat.shape[0] // n_seq
    assert ppb % CMP_PAGES == 0
    phi_h = w1.shape[-1]
    w = jnp.concatenate([w1[:CMP_STRIDE].reshape(CMP_STRIDE * HEAD_DIM, phi_h),
                         w1[CMP_STRIDE:].reshape(CMP_STRIDE * HEAD_DIM, phi_h)], axis=1).astype(BF16)
    steps = ppb // CMP_PAGES
    per_g = CMP_PAGES * CHUNKS_PER_PAGE

    def page_spec(i):
        return pl.BlockSpec((1, 1, PAGE_SIZE * NSA_KV_GROUPS, HEAD_DIM),
                            lambda b, j, pt: (layer, pt[b * ppb + j * CMP_PAGES + i], 0, 0))

    return pl.pallas_call(
        _compress_kernel,
        grid_spec=pltpu.PrefetchScalarGridSpec(
            num_scalar_prefetch=1, grid=(n_seq, steps),
            in_specs=[page_spec(i) for i in range(CMP_PAGES)]
                     + [pl.BlockSpec(w.shape, lambda b, j, pt: (0, 0))],
            out_specs=pl.BlockSpec((1, NSA_KV_GROUPS, per_g, 2 * phi_h), lambda b, j, pt: (b, 0, j, 0))),
        out_shape=jax.ShapeDtypeStruct((n_seq, NSA_KV_GROUPS, ppb * CHUNKS_PER_PAGE, 2 * phi_h), F32),
        compiler_params=pltpu.CompilerParams(dimension_semantics=("parallel", "arbitrary"),
                                             vmem_limit_bytes=V7X_VMEM_LIMIT_BYTES),
        name="compress_first",
    )(page_flat, *([pool] * CMP_PAGES), w)


def compress_finish(fs, w1, w2, pe):
    phi_h = w1.shape[-1]
    pe_bias = jnp.einsum('ld,lde->e', pe, w1)
    hid = jax.nn.gelu(fs[:, :, :-1, :phi_h] + fs[:, :, 1:, phi_h:] + pe_bias)
    b, G, n, _ = hid.shape
    assert (b * G) % 8 == 0
    return pmm(hid.reshape(b * G * n, phi_h), w2, tm=8 * n).reshape(b, G, n, w2.shape[-1])


NSA_TQ = 128
NSA_TK = 256
CB_PAD = 128


def _nt_dot(a, b):
    return lax.dot_general(a, b, (((1,), (1,)), ((), ())), preferred_element_type=F32)


def _split3_dot(x, w):
    hi = x.astype(BF16)
    r1 = x - hi.astype(F32)
    mid = r1.astype(BF16)
    lo = (r1 - mid.astype(F32)).astype(BF16)
    return (jnp.dot(hi, w, preferred_element_type=F32) + jnp.dot(mid, w, preferred_element_type=F32)
            + jnp.dot(lo, w, preferred_element_type=F32))


def _flash_tiles(q4, k_ref, v_ref, lo, hi, mask_fn):
    rows = q4.shape[0]

    def body(jt, carry):
        m, l, acc = carry
        k0 = pl.multiple_of(jt * NSA_TK, NSA_TK)
        kt = k_ref[0, pl.ds(k0, NSA_TK), :].astype(BF16)
        vt = v_ref[0, pl.ds(k0, NSA_TK), :].astype(BF16)
        s = _nt_dot(q4, kt) * ATTN_SCALE
        mk = mask_fn(k0)
        mk4 = jnp.concatenate([mk] * NSA_REP, axis=0) > 0.5
        sm = jnp.where(mk4, s, NEG_INF)
        m_new = jnp.maximum(m, jnp.max(sm, axis=-1, keepdims=True))
        alpha = jnp.exp(m - m_new)
        e = jnp.where(mk4, jnp.exp(sm - m_new), 0.0)
        l = alpha * l + jnp.sum(e, axis=-1, keepdims=True)
        acc = alpha * acc + jnp.dot(e.astype(BF16), vt, preferred_element_type=F32)
        return m_new, l, acc

    init = (jnp.full((rows, 1), NEG_INF, F32), jnp.zeros((rows, 1), F32), jnp.zeros((rows, HEAD_DIM), F32))
    _, l, acc = lax.fori_loop(lo, hi, body, init)
    return acc / jnp.where(l > 0.0, l, 1.0)


def _nsa_prompt_kernel(q_ref, kc_ref, vc_ref, ks_ref, vs_ref, kw_ref, vw_ref, gate_ref, c2s_ref, o_ref,
                       *, n_cb, n_sb):
    tq = NSA_TQ
    q0 = pl.program_id(2) * tq
    qb = q_ref[0]
    q4 = jnp.concatenate([qb[:, r * HEAD_DIM:(r + 1) * HEAD_DIM] for r in range(NSA_REP)], axis=0)

    s = _nt_dot(q4, kc_ref[0, 0]) * ATTN_SCALE
    row = lax.broadcasted_iota(jnp.int32, s.shape, 0)
    tpos = q0 + (row & (tq - 1))
    n = lax.broadcasted_iota(jnp.int32, s.shape, 1)
    cmask = (n * CMP_STRIDE + (CMP_LEN - 1) <= tpos) & (n < n_cb)
    sm = jnp.where(cmask, s, NEG_INF)
    m_c = jnp.max(sm, axis=-1, keepdims=True)
    e = jnp.where(cmask, jnp.exp(sm - m_c), 0.0)
    l_c = jnp.sum(e, axis=-1, keepdims=True)
    p = e / jnp.where(l_c > 0.0, l_c, 1.0)
    o_c = jnp.dot(p.astype(BF16), vc_ref[0, 0], preferred_element_type=F32)

    psum = p[0:tq] + p[tq:2 * tq] + p[2 * tq:3 * tq] + p[3 * tq:4 * tq]
    imp_t = _split3_dot(psum, c2s_ref[...]).T
    imp_t = imp_t[:n_sb]
    j = lax.broadcasted_iota(jnp.int32, imp_t.shape, 0)
    t = q0 + lax.broadcasted_iota(jnp.int32, imp_t.shape, 1)
    cur = jnp.right_shift(t, SEL_SHIFT)
    valid = j * SEL_BLOCK <= t
    forced = (j == 0) | (j == cur) | (j == cur - 1)
    score = jnp.where(valid, imp_t + jnp.where(forced, FORCE_BONUS, 0.0), NEG_INF)
    rank = jnp.zeros(score.shape, jnp.int32)
    for i in range(n_sb):
        si = score[i:i + 1, :]
        beats = (si > score) | ((si == score) & (i < j))
        rank = rank + jnp.where(beats, 1, 0)
    sel_t = jnp.where((rank < N_SEL) & (score > 0.5 * NEG_INF), 1.0, 0.0)
    sel_t = jnp.concatenate([sel_t, jnp.zeros((CB_PAD - n_sb, tq), F32)], axis=0)
    sel = sel_t.T.astype(BF16)

    def slc_mask(k0):
        jb = lax.broadcasted_iota(jnp.int32, (CB_PAD, NSA_TK), 0)
        kc = k0 + lax.broadcasted_iota(jnp.int32, (CB_PAD, NSA_TK), 1)
        expand = jnp.where(jb == jnp.right_shift(kc, SEL_SHIFT), 1.0, 0.0).astype(BF16)
        selx = jnp.dot(sel, expand, preferred_element_type=F32)
        qpos = q0 + lax.broadcasted_iota(jnp.int32, (tq, NSA_TK), 0)
        kpos = k0 + lax.broadcasted_iota(jnp.int32, (tq, NSA_TK), 1)
        return jnp.where((selx > 0.5) & (kpos <= qpos), 1.0, 0.0)

    def win_mask(k0):
        qpos = q0 + lax.broadcasted_iota(jnp.int32, (tq, NSA_TK), 0)
        kpos = k0 + lax.broadcasted_iota(jnp.int32, (tq, NSA_TK), 1)
        d = qpos - kpos
        return jnp.where((d >= 0) & (d < WINDOW), 1.0, 0.0)

    hi = (q0 + tq - 1) // NSA_TK + 1
    o_s = _flash_tiles(q4, ks_ref, vs_ref, 0, hi, slc_mask)
    o_w = _flash_tiles(q4, kw_ref, vw_ref, jnp.maximum(q0 - (WINDOW - 1), 0) // NSA_TK, hi, win_mask)

    g = jax.nn.sigmoid(gate_ref[0, 0])
    for r in range(NSA_REP):
        rs = slice(r * tq, (r + 1) * tq)
        o = (g[:, r:r + 1] * o_c[rs] + g[:, NSA_REP + r:NSA_REP + r + 1] * o_s[rs]
             + g[:, 2 * NSA_REP + r:2 * NSA_REP + r + 1] * o_w[rs])
        o_ref[0, :, r * HEAD_DIM:(r + 1) * HEAD_DIM] = o.astype(o_ref.dtype)


def nsa_prompt_attention(q, kc, vc, k_slc, v_slc, k_win, v_win, g_nsa):
    b, T, _ = q.shape
    n_cb = kc.shape[2]
    n_sb = T // SEL_BLOCK
    assert T % NSA_TK == 0 and n_cb <= CB_PAD and n_sb <= CB_PAD and n_sb % 8 == 0
    G = NSA_KV_GROUPS

    def pack_c(x):
        return jnp.pad(x, ((0, 0), (0, 0), (0, CB_PAD - n_cb), (0, 0))).astype(BF16)

    gates = jnp.transpose(g_nsa.reshape(b, T, 3, G, NSA_REP), (0, 3, 1, 2, 4)).reshape(b, G, T, 3 * NSA_REP)
    i = np.arange(CB_PAD)[:, None] * CMP_STRIDE
    jj = np.arange(CB_PAD)[None, :] * SEL_BLOCK
    c2s = ((i < jj + SEL_BLOCK) & (i + CMP_LEN > jj) & (np.arange(CB_PAD)[:, None] < n_cb)
           & (np.arange(CB_PAD)[None, :] < n_sb))
    c2s = jnp.asarray(c2s, BF16)
    flat = lambda x: x.reshape(b, T, G * HEAD_DIM)
    kv_spec = pl.BlockSpec((1, T, HEAD_DIM), lambda bi, g, qi: (bi, 0, g))
    c_spec = pl.BlockSpec((1, 1, CB_PAD, HEAD_DIM), lambda bi, g, qi: (bi, g, 0, 0))
    return pl.pallas_call(
        functools.partial(_nsa_prompt_kernel, n_cb=n_cb, n_sb=n_sb),
        grid=(b, G, T // NSA_TQ),
        in_specs=[pl.BlockSpec((1, NSA_TQ, NSA_REP * HEAD_DIM), lambda bi, g, qi: (bi, qi, g)),
                  c_spec, c_spec, kv_spec, kv_spec, kv_spec, kv_spec,
                  pl.BlockSpec((1, 1, NSA_TQ, 3 * NSA_REP), lambda bi, g, qi: (bi, g, qi, 0)),
                  pl.BlockSpec((CB_PAD, CB_PAD), lambda bi, g, qi: (0, 0))],
        out_specs=pl.BlockSpec((1, NSA_TQ, NSA_REP * HEAD_DIM), lambda bi, g, qi: (bi, qi, g)),
        out_shape=jax.ShapeDtypeStruct((b, T, NSA_HEADS * HEAD_DIM), BF16),
        compiler_params=pltpu.CompilerParams(
            dimension_semantics=("parallel", "parallel", "arbitrary"),
            vmem_limit_bytes=V7X_VMEM_LIMIT_BYTES),
        name="nsa_prompt",
    )(q, pack_c(kc), pack_c(vc), flat(k_slc), flat(v_slc), flat(k_win), flat(v_win), gates, c2s)


SSD_GW = SSM_REP * SSM_HEAD_DIM
DT_PAD = 128


def _split3(x):
    hi = x.astype(BF16)
    r1 = x - hi.astype(F32)
    mid = r1.astype(BF16)
    lo = (r1 - mid.astype(F32)).astype(BF16)
    return hi, mid, lo


def _exact_left(w, x):
    hi, mid, lo = _split3(x)
    return (jnp.dot(w, hi, preferred_element_type=F32) + jnp.dot(w, mid, preferred_element_type=F32)
            + jnp.dot(w, lo, preferred_element_type=F32))


def _ssd_prompt_kernel(xbc_ref, z_ref, dt_ref, cw_ref, cb_ref, dtb_ref, aneg_ref, dsk_ref, nw_ref,
                       expand_ref, tril_ref, y_ref, hfin_ref, xprev_ref, ht_ref):
    c = pl.program_id(1)
    Q = SSD_CHUNK

    @pl.when(c == 0)
    def _():
        xprev_ref[...] = jnp.zeros_like(xprev_ref)
        ht_ref[...] = jnp.zeros_like(ht_ref)

    x = xbc_ref[0]
    xp = xprev_ref[...]
    rowi = lax.broadcasted_iota(jnp.int32, x.shape, 0)
    acc = jnp.broadcast_to(cb_ref[...], x.shape)
    for k in range(CONV_W):
        s = CONV_W - 1 - k
        if s == 0:
            xs_k = x
        else:
            xs_k = jnp.where(rowi >= s, pltpu.roll(x, s, 0), pltpu.roll(xp, s, 0))
        acc = acc + xs_k * cw_ref[k:k + 1, :]
    xprev_ref[...] = x
    xc = acc * jax.nn.sigmoid(acc)
    xs = xc[:, :D_INNER]

    lane = lax.broadcasted_iota(jnp.int32, (Q, DT_PAD), 1)
    dtin = dt_ref[0] + dtb_ref[...]
    dt = jnp.maximum(dtin, 0.0) + jnp.log1p(jnp.exp(-jnp.abs(dtin)))
    dt = jnp.where(lane < SSM_HEADS, dt, 0.0)
    acum = _exact_left(tril_ref[...], dt * aneg_ref[...])
    acum_t = acum.T
    eacum = jnp.exp(acum)
    decay = jnp.exp(acum[Q - 1:Q, :] - acum) * dt
    expand = expand_ref[...]
    dt_e = _split3_dot(dt, expand)
    eacum_e = _split3_dot(eacum, expand)
    decay_e = _split3_dot(decay, expand)
    dtx = (dt_e * xs).astype(BF16)
    dxs = (decay_e * xs).astype(BF16)

    li = lax.broadcasted_iota(jnp.int32, (Q, Q), 0)
    si = lax.broadcasted_iota(jnp.int32, (Q, Q), 1)
    causal = li >= si
    half = lax.broadcasted_iota(jnp.int32, (Q, 2 * SSM_HEAD_DIM), 1) < SSM_HEAD_DIM

    for g in range(SSM_GROUPS):
        gs = slice(g * SSD_GW, (g + 1) * SSD_GW)
        bg = xc[:, D_INNER + g * D_STATE:D_INNER + (g + 1) * D_STATE]
        cg = xc[:, D_INNER + SSM_GROUPS * D_STATE + g * D_STATE:D_INNER + SSM_GROUPS * D_STATE + (g + 1) * D_STATE]
        bg16, cg16 = bg.astype(BF16), cg.astype(BF16)
        cbm = _nt_dot(cg16, bg16)
        ht = ht_ref[g]
        y = jnp.dot(cg16, ht.astype(BF16), preferred_element_type=F32) * eacum_e[:, gs]
        pieces = []
        for i in range(SSM_REP // 2):
            ws = []
            for h in (g * SSM_REP + 2 * i, g * SSM_REP + 2 * i + 1):
                seg = acum[:, h:h + 1] - acum_t[h:h + 1, :]
                ws.append((cbm * jnp.exp(jnp.where(causal, seg, NEG_INF))).astype(BF16))
            dpair = dtx[:, g * SSD_GW + i * 2 * SSM_HEAD_DIM:g * SSD_GW + (i + 1) * 2 * SSM_HEAD_DIM]
            pieces.append(jnp.where(half, jnp.dot(ws[0], dpair, preferred_element_type=F32),
                                    jnp.dot(ws[1], dpair, preferred_element_type=F32)))
        y = jnp.concatenate(pieces, axis=1) + y + dsk_ref[:, gs] * xs[:, gs]
        zg = z_ref[0, :, gs]
        y = y * (zg * jax.nn.sigmoid(zg))
        y = y * lax.rsqrt(jnp.mean(y * y, axis=-1, keepdims=True) + RMS_EPS)
        y_ref[0, :, gs] = (y * nw_ref[:, gs]).astype(y_ref.dtype)
        ht_ref[g] = eacum_e[Q - 1:Q, gs] * ht + jnp.dot(bg.T.astype(BF16), dxs[:, gs], preferred_element_type=F32)

    @pl.when(c == pl.num_programs(1) - 1)
    def _():
        for g in range(SSM_GROUPS):
            hfin_ref[0, g * SSD_GW:(g + 1) * SSD_GW, :] = ht_ref[g].T


def ssd_prompt(xbc, z, dt_raw, conv_w, conv_b, dt_bias, a_log, d_skip, norm_w):
    b, T, _ = xbc.shape
    assert T % SSD_CHUNK == 0 and dt_raw.shape[-1] == DT_PAD
    pad = lambda v: jnp.pad(v.astype(F32), (0, DT_PAD - SSM_HEADS)).reshape(1, DT_PAD)
    head_of = np.arange(D_INNER) // SSM_HEAD_DIM
    expand = jnp.asarray(np.arange(DT_PAD)[:, None] == head_of[None, :], BF16)
    tril = jnp.asarray(np.tril(np.ones((SSD_CHUNK, SSD_CHUNK))), BF16)
    dsk = jnp.repeat(d_skip.astype(F32), SSM_HEAD_DIM).reshape(1, D_INNER)
    const = lambda shape: pl.BlockSpec(shape, lambda bi, c: (0,) * len(shape))
    y, hfin = pl.pallas_call(
        _ssd_prompt_kernel,
        grid=(b, T // SSD_CHUNK),
        in_specs=[pl.BlockSpec((1, SSD_CHUNK, CONV_DIM), lambda bi, c: (bi, c, 0)),
                  pl.BlockSpec((1, SSD_CHUNK, D_INNER), lambda bi, c: (bi, c, 0)),
                  pl.BlockSpec((1, SSD_CHUNK, DT_PAD), lambda bi, c: (bi, c, 0)),
                  const((CONV_W, CONV_DIM)), const((1, CONV_DIM)), const((1, DT_PAD)), const((1, DT_PAD)),
                  const((1, D_INNER)), const((1, D_INNER)), const((DT_PAD, D_INNER)),
                  const((SSD_CHUNK, SSD_CHUNK))],
        out_specs=[pl.BlockSpec((1, SSD_CHUNK, D_INNER), lambda bi, c: (bi, c, 0)),
                   pl.BlockSpec((1, D_INNER, D_STATE), lambda bi, c: (bi, 0, 0))],
        out_shape=[jax.ShapeDtypeStruct((b, T, D_INNER), BF16),
                   jax.ShapeDtypeStruct((b, D_INNER, D_STATE), F32)],
        scratch_shapes=[pltpu.VMEM((SSD_CHUNK, CONV_DIM), F32),
                        pltpu.VMEM((SSM_GROUPS, D_STATE, SSD_GW), F32)],
        compiler_params=pltpu.CompilerParams(dimension_semantics=("parallel", "arbitrary"),
                                             vmem_limit_bytes=V7X_VMEM_LIMIT_BYTES),
        name="ssd_prompt",
    )(xbc, z, dt_raw, conv_w.astype(F32), conv_b.astype(F32).reshape(1, CONV_DIM), pad(dt_bias),
      pad(-jnp.exp(a_log.astype(F32))), dsk, norm_w.astype(F32).reshape(1, D_INNER), expand, tril)
    return y, hfin.reshape(b, SSM_HEADS, SSM_HEAD_DIM, D_STATE)


MOE_TM = 1024
MOE_TN = 512
MOE_TK = 512


def _moe_up_kernel(te_ref, nu_ref, x_ref, wg_ref, wu_ref, h_ref):
    @pl.when(pl.program_id(0) < nu_ref[0])
    def _():
        x = x_ref[...]
        g = jnp.dot(x, wg_ref[0].astype(BF16), preferred_element_type=F32)
        u = jnp.dot(x, wu_ref[0].astype(BF16), preferred_element_type=F32)
        h_ref[...] = (g * jax.nn.sigmoid(g) * u).astype(h_ref.dtype)


def _moe_down_kernel(te_ref, nu_ref, h_ref, wd_ref, gw_ref, y_ref):
    k = pl.program_id(1)

    @pl.when(pl.program_id(0) < nu_ref[0])
    def _():
        part = jnp.dot(h_ref[...], wd_ref[0].astype(BF16), preferred_element_type=F32)

        @pl.when(k == 0)
        def _():
            y_ref[...] = part

        @pl.when(k > 0)
        def _():
            y_ref[...] += part

        @pl.when(k == pl.num_programs(1) - 1)
        def _():
            y_ref[...] = y_ref[...] * gw_ref[...]


def moe_grouped(x_sorted, row_w, tile_expert, n_used, wg, wu, wd, tm):
    P, D = x_sorted.shape
    E, _, FF = wg.shape
    tn = min(MOE_TN, FF)
    tk = min(MOE_TK, FF)
    assert P % tm == 0 and FF % tn == 0 and FF % tk == 0
    nt, nj, nk = P // tm, FF // tn, FF // tk

    def row(i, nu):
        return jnp.minimum(i, nu[0] - 1)

    def col(i, j, nu, last):
        return jnp.where(i < nu[0], j, last)

    h = pl.pallas_call(
        _moe_up_kernel,
        grid_spec=pltpu.PrefetchScalarGridSpec(
            num_scalar_prefetch=2, grid=(nt, nj),
            in_specs=[pl.BlockSpec((tm, D), lambda i, j, te, nu: (row(i, nu), 0)),
                      pl.BlockSpec((1, D, tn), lambda i, j, te, nu: (te[row(i, nu)], 0, col(i, j, nu, nj - 1))),
                      pl.BlockSpec((1, D, tn), lambda i, j, te, nu: (te[row(i, nu)], 0, col(i, j, nu, nj - 1)))],
            out_specs=pl.BlockSpec((tm, tn), lambda i, j, te, nu: (row(i, nu), col(i, j, nu, nj - 1)))),
        out_shape=jax.ShapeDtypeStruct((P, FF), BF16),
        compiler_params=pltpu.CompilerParams(dimension_semantics=("arbitrary", "arbitrary"),
                                             vmem_limit_bytes=V7X_VMEM_LIMIT_BYTES),
        name="moe_up",
    )(tile_expert, n_used, x_sorted, wg, wu)
    return pl.pallas_call(
        _moe_down_kernel,
        grid_spec=pltpu.PrefetchScalarGridSpec(
            num_scalar_prefetch=2, grid=(nt, nk),
            in_specs=[pl.BlockSpec((tm, tk), lambda i, k, te, nu: (row(i, nu), col(i, k, nu, nk - 1))),
                      pl.BlockSpec((1, tk, D), lambda i, k, te, nu: (te[row(i, nu)], col(i, k, nu, nk - 1), 0)),
                      pl.BlockSpec((tm, 1), lambda i, k, te, nu: (row(i, nu), 0))],
            out_specs=pl.BlockSpec((tm, D), lambda i, k, te, nu: (row(i, nu), 0))),
        out_shape=jax.ShapeDtypeStruct((P, D), F32),
        compiler_params=pltpu.CompilerParams(dimension_semantics=("arbitrary", "arbitrary"),
                                             vmem_limit_bytes=V7X_VMEM_LIMIT_BYTES),
        name="moe_down",
    )(tile_expert, n_used, h, wd, row_w)


def moe_topk(u, w_router, wg, wu, wd, tm=MOE_TM):
    N, D = u.shape
    E = wg.shape[0]
    logits = jnp.dot(u, w_router, precision=lax.Precision.HIGHEST)
    top_l, top_i = lax.top_k(logits, TOP_K)
    top_w = jax.nn.softmax(top_l, axis=-1)
    flat_e = top_i.reshape(-1)
    onehot = (flat_e[:, None] == jnp.arange(E, dtype=flat_e.dtype)[None, :]).astype(jnp.int32)
    csum = jnp.cumsum(onehot, axis=0)
    rank = jnp.take_along_axis(csum, flat_e[:, None], axis=1)[:, 0] - 1
    tiles_per_e = (csum[-1] + tm - 1) // tm
    tile_end = jnp.cumsum(tiles_per_e)
    dest = (tile_end - tiles_per_e)[flat_e] * tm + rank
    nt = (TOP_K * N + E * (tm - 1)) // tm + 1
    P = nt * tm
    tile_expert = jnp.minimum(jnp.searchsorted(tile_end, jnp.arange(nt), side='right'), E - 1).astype(jnp.int32)
    n_used = tile_end[-1:].astype(jnp.int32)
    row_token = jnp.zeros((P,), jnp.int32).at[dest].set(jnp.arange(TOP_K * N, dtype=jnp.int32) // TOP_K)
    row_w = jnp.zeros((P, 1), F32).at[dest, 0].set(top_w.reshape(-1))
    x_sorted = u.astype(BF16)[row_token]
    y_sorted = moe_grouped(x_sorted, row_w, tile_expert, n_used, wg, wu, wd, tm)
    picked = y_sorted[dest].reshape(N, TOP_K, D)
    y = picked[:, 0]
    for k in range(1, TOP_K):
        y = y + picked[:, k]
    return y


def rmsnorm(x, w):
    xf = x.astype(F32)
    y = xf * lax.rsqrt(jnp.mean(xf * xf, axis=-1, keepdims=True) + RMS_EPS)
    return (y * w.astype(F32)).astype(x.dtype)


def masked_softmax(s, mask):
    p = jax.nn.softmax(jnp.where(mask, s, NEG_INF), axis=-1)
    return p * mask


def rotary(x, pos):
    half = ROT_DIM // 2
    inv = ROPE_THETA ** (-jnp.arange(half, dtype=F32) / half)
    ang = pos.astype(F32)[:, None] * inv[None, :]
    ang = ang.reshape((1, pos.shape[0]) + (1,) * (x.ndim - 3) + (half,))
    cos, sin = jnp.cos(ang), jnp.sin(ang)
    xr = x[..., :ROT_DIM].astype(F32)
    x1, x2 = xr[..., :half], xr[..., half:]
    rot = jnp.concatenate([x1 * cos - x2 * sin, x2 * cos + x1 * sin], axis=-1)
    return jnp.concatenate([rot.astype(x.dtype), x[..., ROT_DIM:]], axis=-1)


def split_in(proj):
    cuts = np.cumsum(IN_SIZES)[:-1].tolist()
    return jnp.split(proj, cuts, axis=-1)


def ssd_chunked(x, dt, a, bm, cm, h0):
    b, T = x.shape[:2]
    q = SSD_CHUNK if T >= SSD_CHUNK else T
    nc = -(-T // q)
    pad = nc * q - T

    def chunks(v):
        v = jnp.pad(v.astype(F32), [(0, 0), (0, pad)] + [(0, 0)] * (v.ndim - 2))
        return jnp.moveaxis(v.reshape((b, nc, q) + v.shape[2:]), 1, 0)

    causal = jnp.tril(jnp.ones((q, q), bool))[None, :, :, None, None]

    def step(h, inp):
        xc, dtc, bc, cc = inp
        acum = jnp.cumsum(dtc * a, axis=1)
        seg = acum[:, :, None] - acum[:, None, :]
        lmat = jnp.exp(jnp.where(causal, seg, -jnp.inf))
        cb = jnp.einsum('blgn,bsgn->blsg', cc, bc)
        y_diag = jnp.einsum('blsg,blsgr,bsgrp->blgrp', cb, lmat, dtc[..., None] * xc)
        y_off = jnp.einsum('blgn,bgrpn->blgrp', cc, h) * jnp.exp(acum)[..., None]
        decay = jnp.exp(acum[:, -1:] - acum) * dtc
        h_new = jnp.exp(acum[:, -1])[..., None, None] * h + jnp.einsum('bsgn,bsgr,bsgrp->bgrpn', bc, decay, xc)
        return h_new, y_diag + y_off

    h_fin, ys = lax.scan(step, h0.astype(F32), (chunks(x), chunks(dt), chunks(bm), chunks(cm)))
    y = jnp.moveaxis(ys, 0, 1).reshape((b, nc * q) + x.shape[2:])[:, :T]
    return y, h_fin


def ssm_branch(z, xbc, dt_raw, conv0, h0, conv_w, conv_b, dt_bias, a_log, d_skip, norm_w):
    b, T, _ = xbc.shape
    xpad = jnp.concatenate([conv0.astype(xbc.dtype), xbc], axis=1)
    acc = conv_b
    for k in range(CONV_W):
        acc = acc + xpad[:, k:k + T] * conv_w[k]
    xbc_c = jax.nn.silu(acc)
    new_conv = xpad[:, T:]
    xs, bm, cm = jnp.split(xbc_c, [D_INNER, D_INNER + SSM_GROUPS * D_STATE], axis=-1)
    xs = xs.reshape(b, T, SSM_GROUPS, SSM_REP, SSM_HEAD_DIM)
    bm = bm.reshape(b, T, SSM_GROUPS, D_STATE)
    cm = cm.reshape(b, T, SSM_GROUPS, D_STATE)
    dt = jax.nn.softplus(dt_raw.astype(F32) + dt_bias.astype(F32)).reshape(b, T, SSM_GROUPS, SSM_REP)
    a = -jnp.exp(a_log.astype(F32)).reshape(SSM_GROUPS, SSM_REP)
    h0 = h0.reshape(b, SSM_GROUPS, SSM_REP, SSM_HEAD_DIM, D_STATE)
    y, h_fin = ssd_chunked(xs, dt, a, bm, cm, h0)
    y = y + d_skip.astype(F32).reshape(SSM_GROUPS, SSM_REP)[..., None] * xs.astype(F32)
    y = y.reshape(b, T, D_INNER) * jax.nn.silu(z.astype(F32))
    yg = y.reshape(b, T, SSM_GROUPS, D_INNER // SSM_GROUPS)
    yg = yg * lax.rsqrt(jnp.mean(yg * yg, axis=-1, keepdims=True) + RMS_EPS)
    y = yg.reshape(b, T, D_INNER) * norm_w.astype(F32)
    return y.astype(z.dtype), new_conv, h_fin.reshape(b, SSM_HEADS, SSM_HEAD_DIM, D_STATE)


def compress(rows, w1, w2, pe):
    b, L = rows.shape[:2]
    nf = L // CMP_STRIDE
    ch = rows[:, :nf * CMP_STRIDE].reshape(b, nf, CMP_STRIDE, NSA_KV_GROUPS, HEAD_DIM)
    first = jnp.einsum('bnsgd,sde->bnge', ch, w1[:CMP_STRIDE])
    second = jnp.einsum('bnsgd,sde->bnge', ch, w1[CMP_STRIDE:])
    pe_bias = jnp.einsum('ld,lde->e', pe, w1)
    hid = jax.nn.gelu(first[:, :-1] + second[:, 1:] + pe_bias)
    return jnp.einsum('bnge,ed->bngd', hid, w2)


def cmp_to_sel(n_cb, n_sb):
    i = jnp.arange(n_cb)[:, None] * CMP_STRIDE
    j = jnp.arange(n_sb)[None, :] * SEL_BLOCK
    return ((i < j + SEL_BLOCK) & (i + CMP_LEN > j)).astype(F32)


def cmp_attend(q, kc, vc, q_pos):
    n = kc.shape[1]
    s = jnp.einsum('btgrd,bngd->bgrtn', q, kc, preferred_element_type=F32) * ATTN_SCALE
    end = jnp.arange(n) * CMP_STRIDE + CMP_LEN - 1
    p = masked_softmax(s, end[None, :] <= q_pos[:, None])
    o = jnp.einsum('bgrtn,bngd->btgrd', p.astype(vc.dtype), vc)
    return o, p


def select_blocks(imp, q_pos):
    n_sb = imp.shape[-1]
    j = jnp.arange(n_sb)[None, :]
    cur = (q_pos // SEL_BLOCK)[:, None]
    valid = (j * SEL_BLOCK <= q_pos[:, None])[:, None, :]
    forced = ((j == 0) | (j == cur) | (j == cur - 1))[:, None, :]
    score = jnp.where(valid, imp + jnp.where(forced, FORCE_BONUS, 0.0), NEG_INF)
    top, idx = lax.top_k(score, min(N_SEL, n_sb))
    return idx, top > 0.5 * NEG_INF


def slc_attend(q, kb, vb, idx, sel_ok, q_pos):
    kpos = idx[..., None] * SEL_BLOCK + jnp.arange(SEL_BLOCK)
    mask = sel_ok[..., None] & (kpos <= q_pos[None, :, None, None, None])
    s = jnp.einsum('btgrd,btgkld->btgrkl', q, kb, preferred_element_type=F32) * ATTN_SCALE
    b, T, G, R, K, L = s.shape
    p = masked_softmax(s.reshape(b, T, G, R, K * L), mask.reshape(b, T, G, 1, K * L))
    return jnp.einsum('btgrkl,btgkld->btgrd', p.reshape(s.shape).astype(vb.dtype), vb)


def slc_prompt(q, k, v, idx, sel_ok, q_pos):
    b, T = q.shape[:2]
    nsb = T // SEL_BLOCK
    kblk = k.reshape(b, nsb, SEL_BLOCK, NSA_KV_GROUPS, HEAD_DIM)
    vblk = v.reshape(b, nsb, SEL_BLOCK, NSA_KV_GROUPS, HEAD_DIM)
    bi = jnp.arange(b)[:, None, None, None]
    gi = jnp.arange(NSA_KV_GROUPS)[None, None, :, None]
    nq = T // SLC_QCHUNK

    def chunked(a):
        return jnp.moveaxis(a.reshape((b, nq, SLC_QCHUNK) + a.shape[2:]), 1, 0)

    def body(args):
        qc, ic, okc, pc = args
        kb = kblk[bi, ic, :, gi, :]
        vb = vblk[bi, ic, :, gi, :]
        return slc_attend(qc, kb, vb, ic, okc, pc)

    out = lax.map(body, (chunked(q), chunked(idx), chunked(sel_ok), q_pos.reshape(nq, SLC_QCHUNK)))
    return jnp.moveaxis(out, 0, 1).reshape(q.shape)


def gather_selected(pool, new_rows, idx, page_table, past):
    b, S = new_rows.shape[:2]
    bpp = PAGE_SIZE // SEL_BLOCK
    n_past_blk = past // SEL_BLOCK
    n_new_blk = -(-S // SEL_BLOCK)
    bi = jnp.arange(b)[:, None, None, None]
    gi = jnp.arange(NSA_KV_GROUPS)[None, None, :, None]
    jp = jnp.minimum(idx, n_past_blk - 1)
    phys = page_table[bi, jp // bpp]
    pool_blk = pool.reshape(pool.shape[0], bpp, SEL_BLOCK, NSA_KV_GROUPS, HEAD_DIM)
    from_past = pool_blk[phys, jp % bpp, :, gi, :]
    new_blk = jnp.pad(new_rows, ((0, 0), (0, n_new_blk * SEL_BLOCK - S), (0, 0), (0, 0)))
    new_blk = new_blk.reshape(b, n_new_blk, SEL_BLOCK, NSA_KV_GROUPS, HEAD_DIM)
    jn = jnp.clip(idx - n_past_blk, 0, n_new_blk - 1)
    from_new = new_blk[bi, jn, :, gi, :]
    return jnp.where((idx < n_past_blk)[..., None, None], from_past, from_new.astype(from_past.dtype))


def window_attend(q, k, v, q_pos, k_pos):
    s = jnp.einsum('btgrd,bsgd->bgrts', q, k, preferred_element_type=F32) * ATTN_SCALE
    diff = q_pos[:, None] - k_pos[None, :]
    mask = (diff >= 0) & (diff < WINDOW) & (k_pos[None, :] >= 0)
    p = masked_softmax(s, mask)
    return jnp.einsum('bgrts,bsgd->btgrd', p.astype(v.dtype), v)


def win_prompt(q, k, v):
    b, T = q.shape[:2]
    nq = T // WIN_QBLOCK
    span = WIN_QBLOCK + WINDOW
    kp = jnp.pad(k, ((0, 0), (WINDOW, 0), (0, 0), (0, 0)))
    vp = jnp.pad(v, ((0, 0), (WINDOW, 0), (0, 0), (0, 0)))
    qb = jnp.moveaxis(q.reshape((b, nq, WIN_QBLOCK) + q.shape[2:]), 1, 0)

    def body(args):
        i, qc = args
        start = i * WIN_QBLOCK
        kc = lax.dynamic_slice_in_dim(kp, start, span, axis=1)
        vc = lax.dynamic_slice_in_dim(vp, start, span, axis=1)
        q_pos = start + jnp.arange(WIN_QBLOCK)
        k_pos = start - WINDOW + jnp.arange(span)
        return window_attend(qc, kc, vc, q_pos, k_pos)

    out = lax.map(body, (jnp.arange(nq), qb))
    return jnp.moveaxis(out, 0, 1).reshape(q.shape)


def combine_nsa(gate, o_cmp, o_slc, o_win, dtype):
    b, T = o_cmp.shape[:2]
    o = gate[:, :, 0] * o_cmp + gate[:, :, 1] * o_slc + gate[:, :, 2] * o_win
    return o.reshape(b, T, NSA_HEADS * HEAD_DIM).astype(dtype)


def nsa_prompt(q, kvs, g_nsa, phi):
    phi1_k, phi2_k, pe_k, phi1_v, phi2_v, pe_v = phi
    b, T, _ = q.shape
    cos_t, sin_t = rope_tables(jnp.arange(T, dtype=jnp.int32))
    heads = lambda x: x.reshape(b, T, NSA_KV_GROUPS, HEAD_DIM)
    rot = lambda x, dt: rope_rows(x.reshape(b * T, x.shape[-1]), cos_t, sin_t, dt).reshape(x.shape)
    q = rot(q, BF16)
    k_cmp, k_slc, k_win = heads(rot(kvs[0], F32)), heads(rot(kvs[2], F32)), heads(rot(kvs[4], F32))
    v_cmp, v_slc, v_win = heads(kvs[1]), heads(kvs[3]), heads(kvs[5])
    assert T % (PAGE_SIZE * CMP_PAGES) == 0
    pages = jnp.arange(b * T // PAGE_SIZE, dtype=jnp.int32)
    as_pool = lambda x: x.reshape(1, b * T // PAGE_SIZE, PAGE_SIZE * NSA_KV_GROUPS, HEAD_DIM)
    kc = compress_finish(compress_first(as_pool(k_cmp), 0, pages, b, phi1_k), phi1_k, phi2_k, pe_k)
    vc = compress_finish(compress_first(as_pool(v_cmp), 0, pages, b, phi1_v), phi1_v, phi2_v, pe_v)
    o = nsa_prompt_attention(q, kc, vc, k_slc, v_slc, k_win, v_win, g_nsa)
    wb = min(WINDOW, T)
    return o, (k_cmp, v_cmp, k_slc, v_slc, k_win[:, T - wb:], v_win[:, T - wb:])


def nsa_sample(q, kv, g_nsa, phi, layer, pool_k_cmp, pool_v_cmp, ck_slc, cv_slc, ck_win, cv_win, page_table):
    phi1_k, phi2_k, pe_k, phi1_v, phi2_v, pe_v = phi
    b, S = q.shape[:2]
    assert S < CMP_STRIDE
    gate = jax.nn.sigmoid(g_nsa.astype(F32)).reshape(b, S, 3, NSA_KV_GROUPS, NSA_REP, 1)
    past = page_table.shape[1] * PAGE_SIZE
    pos = past + jnp.arange(S, dtype=jnp.int32)
    q = rotary(q, pos)
    k_cmp, k_slc, k_win = rotary(kv[:, :, 0], pos), rotary(kv[:, :, 2], pos), rotary(kv[:, :, 4], pos)
    v_cmp, v_slc, v_win = kv[:, :, 1], kv[:, :, 3], kv[:, :, 5]

    def compress_past(pool, w1, w2, pe):
        pool = pool.reshape(pool.shape[:2] + (PAGE_SIZE * NSA_KV_GROUPS, HEAD_DIM))
        blocks = compress_finish(compress_first(pool, layer, page_table.reshape(-1), b, w1), w1, w2, pe)
        return jnp.transpose(blocks, (0, 2, 1, 3))

    kc = compress_past(pool_k_cmp, phi1_k, phi2_k, pe_k)
    vc = compress_past(pool_v_cmp, phi1_v, phi2_v, pe_v)
    o_cmp, p_cmp = cmp_attend(q, kc, vc, pos)
    imp = jnp.einsum('bgrtn,nj->btgj', p_cmp, cmp_to_sel(kc.shape[1], -(-(past + S) // SEL_BLOCK)))
    idx, sel_ok = select_blocks(imp, pos)
    kb = gather_selected(ck_slc, k_slc, idx, page_table, past)
    vb = gather_selected(cv_slc, v_slc, idx, page_table, past)
    o_slc = slc_attend(q, kb, vb, idx, sel_ok, pos)
    wb = ck_win.shape[1]
    k_all = jnp.concatenate([ck_win.astype(k_win.dtype), k_win], axis=1)
    v_all = jnp.concatenate([cv_win.astype(v_win.dtype), v_win], axis=1)
    k_pos = past - wb + jnp.arange(wb + S)
    o_win = window_attend(q, k_all, v_all, pos, k_pos)
    o = combine_nsa(gate, o_cmp, o_slc, o_win, q.dtype)
    return o, (k_cmp, v_cmp, k_slc, v_slc, k_all[:, -wb:], v_all[:, -wb:])


def token_mixer(h, conv0, ssm0, nsa_fn, norm_w, w_in, conv_w, conv_b, dt_bias, a_log, d_skip,
                ssm_norm_w, w_br_ssm, w_br_nsa, w_out):
    b, T, _ = h.shape
    u = rmsnorm(h, norm_w).astype(BF16)
    o = np.cumsum((0,) + IN_SIZES)
    z, xbc, q, g_nsa, g_merge = (mm(u, w_in[:, o[i]:o[i + 1]]) for i in (0, 1, 3, 5, 6))
    dt_pad = mm(u, w_in[:, o[2]:o[2] + DT_PAD])
    if conv0 is None:
        y_ssm, ssm_new = ssd_prompt(xbc, z, dt_pad, conv_w, conv_b, dt_bias, a_log, d_skip, ssm_norm_w)
        conv_new = xbc[:, T - (CONV_W - 1):]
        kvs = [mm(u, w_in[:, o[4] + i * KV_WIDTH:o[4] + (i + 1) * KV_WIDTH]) for i in range(6)]
        o_nsa, nsa_state = nsa_fn(q, kvs, g_nsa)
    else:
        y_ssm, conv_new, ssm_new = ssm_branch(z, xbc, dt_pad[..., :SSM_HEADS], conv0, ssm0, conv_w, conv_b,
                                              dt_bias, a_log, d_skip, ssm_norm_w)
        q = q.reshape(b, T, NSA_KV_GROUPS, NSA_REP, HEAD_DIM)
        kv = mm(u, w_in[:, o[4]:o[5]]).reshape(b, T, 6, NSA_KV_GROUPS, HEAD_DIM)
        o_nsa, nsa_state = nsa_fn(q, kv, g_nsa)
    rows = lambda x: x.reshape(b * T, x.shape[-1])
    merged = gated_merge(rows(y_ssm), w_br_ssm, rows(o_nsa), w_br_nsa, rows(g_merge))
    out = pmm(merged, w_out).reshape(b, T, D_MODEL)
    return out, nsa_state + (ssm_new.astype(h.dtype), conv_new)


def swiglu(u, wg, wu, wd):
    lead = u.shape[:-1]
    return pmm(swiglu_up(u.reshape(-1, u.shape[-1]), wg, wu), wd).reshape(lead + (wd.shape[-1],))


def channel_mixer(hp, hs, l, norm_w, w_gate, w_up, w_down, w_router, w_gate_e, w_up_e, w_down_e):
    up, us = rmsnorm(hp, norm_w), rmsnorm(hs, norm_w)
    i = l // 2
    if l % 2 == 0:
        return (swiglu(up.astype(BF16), w_gate[i], w_up[i], w_down[i]),
                swiglu(us.astype(BF16), w_gate[i], w_up[i], w_down[i]))
    n_p = up.shape[0] * up.shape[1]
    u_all = jnp.concatenate([up.reshape(n_p, D_MODEL), us.reshape(-1, D_MODEL)], axis=0)
    y = moe_topk(u_all, w_router[i], w_gate_e[i], w_up_e[i], w_down_e[i])
    return y[:n_p].reshape(up.shape), y[n_p:].reshape(us.shape)


def kernel(x_prompt, x_sample, cache_k_cmp, cache_v_cmp, cache_k_slc, cache_v_slc, cache_k_win, cache_v_win, state_ssm, state_conv, page_table, norm_mix, w_in, conv_w, conv_b, dt_bias, a_log, d_skip, ssm_norm, phi1_k, phi2_k, pe_k, phi1_v, phi2_v, pe_v, w_br_ssm, w_br_nsa, w_out, norm_ffn, w_gate, w_up, w_down, w_router, w_gate_e, w_up_e, w_down_e, norm_final):
    hp, hs = x_prompt, x_sample
    bp = x_prompt.shape[0]
    p_states = [[] for _ in range(8)]
    s_states = [[] for _ in range(8)]
    for l in range(DEPTH):
        mw = (norm_mix[l], w_in[l], conv_w[l], conv_b[l], dt_bias[l], a_log[l], d_skip[l],
              ssm_norm[l], w_br_ssm[l], w_br_nsa[l], w_out[l])
        phi = (phi1_k[l], phi2_k[l], pe_k[l], phi1_v[l], phi2_v[l], pe_v[l])
        ffn = (w_gate, w_up, w_down, w_router, w_gate_e, w_up_e, w_down_e)
        out, st = token_mixer(hp, None, None, functools.partial(nsa_prompt, phi=phi), *mw)
        hp = hp + out
        for i in range(8):
            p_states[i].append(st[i])
        nsa_fn = functools.partial(nsa_sample, phi=phi, layer=l, pool_k_cmp=cache_k_cmp, pool_v_cmp=cache_v_cmp,
                                   ck_slc=cache_k_slc[l], cv_slc=cache_v_slc[l],
                                   ck_win=cache_k_win[l], cv_win=cache_v_win[l], page_table=page_table)
        out, st = token_mixer(hs, state_conv[l], state_ssm[l], nsa_fn, *mw)
        hs = hs + out
        for i in range(8):
            s_states[i].append(st[i])
        fp, fs = channel_mixer(hp, hs, l, norm_ffn[l], *ffn)
        hp, hs = hp + fp, hs + fs
    y_prompt = rmsnorm(hp, norm_final)
    y_sample = rmsnorm(hs, norm_final)
    p_out = [jnp.stack(a) for a in p_states]
    s_out = [jnp.stack(a) for a in s_states]
    return (y_prompt, y_sample, *p_out, *s_out)
```

```python
import functools
import math

import jax
import jax.numpy as jnp
import numpy as np
from jax import lax
from jax.experimental import pallas as pl
from jax.experimental.pallas import tpu as pltpu

F32 = jnp.float32
BF16 = jnp.bfloat16

D_MODEL = 2048
DEPTH = 2
PAGE_SIZE = 128
D_INNER = 4096
SSM_HEAD_DIM = 64
SSM_HEADS = 64
SSM_GROUPS = 8
SSM_REP = 8
D_STATE = 128
CONV_W = 4
CONV_DIM = D_INNER + 2 * SSM_GROUPS * D_STATE
SSD_CHUNK = 128
NSA_HEADS = 16
NSA_KV_GROUPS = 4
NSA_REP = 4
HEAD_DIM = 128
KV_WIDTH = NSA_KV_GROUPS * HEAD_DIM
ROT_DIM = HEAD_DIM // 4
ROPE_THETA = 500000.0
ATTN_SCALE = HEAD_DIM ** -0.5
CMP_STRIDE = 16
CMP_LEN = 32
SEL_BLOCK = 64
SEL_SHIFT = 6
N_SEL = 16
WINDOW = 512
WIN_QBLOCK = 128
SLC_QCHUNK = 16
FORCE_BONUS = 1.0e4
NEG_INF = -1.0e30
N_EXPERTS = 8
TOP_K = 2
RMS_EPS = 1e-6
IN_SIZES = (D_INNER, CONV_DIM, SSM_HEADS, NSA_HEADS * HEAD_DIM, 6 * KV_WIDTH, 3 * NSA_HEADS, 2 * D_MODEL)

V7X_VMEM_LIMIT_BYTES = 56 * 1024 * 1024
LANE = 128


def _dot(x, w, precise):
    if precise:
        return jnp.dot(x.astype(F32), w.astype(F32), preferred_element_type=F32,
                       precision=lax.Precision.HIGHEST)
    return jnp.dot(x.astype(BF16), w.astype(BF16), preferred_element_type=F32)


def _mm_kernel(x_ref, w_ref, o_ref, acc_ref, *, precise):
    k = pl.program_id(2)

    @pl.when(k == 0)
    def _():
        acc_ref[...] = jnp.zeros_like(acc_ref)

    acc_ref[...] += _dot(x_ref[...], w_ref[...], precise)

    @pl.when(k == pl.num_programs(2) - 1)
    def _():
        o_ref[...] = acc_ref[...]


def _pick_tk(K, cap=2048):
    best = None
    for t in range(LANE, min(K, cap) + 1, LANE):
        if K % t == 0:
            best = t
    return best if best is not None else K


def pmm(x, w, tm=1024, tn=512, precise=False):
    M, K = x.shape
    K2, N = w.shape
    assert K == K2
    tm = min(tm, M)
    tn = min(tn, N)
    tk = _pick_tk(K)
    assert M % tm == 0
    grid = (M // tm, pl.cdiv(N, tn), K // tk)
    return pl.pallas_call(
        functools.partial(_mm_kernel, precise=precise),
        grid=grid,
        in_specs=[pl.BlockSpec((tm, tk), lambda i, j, k: (i, k)),
                  pl.BlockSpec((tk, tn), lambda i, j, k: (k, j))],
        out_specs=pl.BlockSpec((tm, tn), lambda i, j, k: (i, j)),
        out_shape=jax.ShapeDtypeStruct((M, N), F32),
        scratch_shapes=[pltpu.VMEM((tm, tn), F32)],
        compiler_params=pltpu.CompilerParams(
            dimension_semantics=("parallel", "parallel", "arbitrary"),
            vmem_limit_bytes=V7X_VMEM_LIMIT_BYTES),
        name="pmm",
    )(x, w)


def mm(x, w, precise=False):
    lead = x.shape[:-1]
    return pmm(x.reshape(-1, x.shape[-1]), w, precise=precise).reshape(lead + (w.shape[-1],))


def _merge_kernel(a_ref, wa_ref, b_ref, wb_ref, ga_ref, gb_ref, o_ref, *, precise):
    ya = _dot(a_ref[...], wa_ref[...], precise)
    yb = _dot(b_ref[...], wb_ref[...], precise)
    o_ref[...] = (jax.nn.sigmoid(ga_ref[...]) * ya + jax.nn.sigmoid(gb_ref[...]) * yb).astype(o_ref.dtype)


def gated_merge(a, wa, b, wb, g_merge, tm=1024, tn=256, precise=False):
    M, Ka = a.shape
    Kb = b.shape[1]
    N = wa.shape[1]
    tm, tn = min(tm, M), min(tn, N)
    assert M % tm == 0 and N % tn == 0 and g_merge.shape == (M, 2 * N)
    nj = N // tn
    return pl.pallas_call(
        functools.partial(_merge_kernel, precise=precise),
        grid=(M // tm, nj),
        in_specs=[pl.BlockSpec((tm, Ka), lambda i, j: (i, 0)),
                  pl.BlockSpec((Ka, tn), lambda i, j: (0, j)),
                  pl.BlockSpec((tm, Kb), lambda i, j: (i, 0)),
                  pl.BlockSpec((Kb, tn), lambda i, j: (0, j)),
                  pl.BlockSpec((tm, tn), lambda i, j: (i, j)),
                  pl.BlockSpec((tm, tn), lambda i, j: (i, j + nj))],
        out_specs=pl.BlockSpec((tm, tn), lambda i, j: (i, j)),
        out_shape=jax.ShapeDtypeStruct((M, N), F32 if precise else BF16),
        compiler_params=pltpu.CompilerParams(dimension_semantics=("parallel", "parallel"),
                                             vmem_limit_bytes=V7X_VMEM_LIMIT_BYTES),
        name="gated_merge",
    )(a, wa, b, wb, g_merge, g_merge)


def _swiglu_up_kernel(x_ref, wg_ref, wu_ref, h_ref, *, precise):
    x = x_ref[...]
    g = _dot(x, wg_ref[...], precise)
    u = _dot(x, wu_ref[...], precise)
    h_ref[...] = (g * jax.nn.sigmoid(g) * u).astype(h_ref.dtype)


def swiglu_up(x, wg, wu, tm=1024, tn=512, precise=False):
    M, D = x.shape
    FF = wg.shape[1]
    tm, tn = min(tm, M), min(tn, FF)
    assert M % tm == 0 and FF % tn == 0
    return pl.pallas_call(
        functools.partial(_swiglu_up_kernel, precise=precise),
        grid=(M // tm, FF // tn),
        in_specs=[pl.BlockSpec((tm, D), lambda i, j: (i, 0)),
                  pl.BlockSpec((D, tn), lambda i, j: (0, j)),
                  pl.BlockSpec((D, tn), lambda i, j: (0, j))],
        out_specs=pl.BlockSpec((tm, tn), lambda i, j: (i, j)),
        out_shape=jax.ShapeDtypeStruct((M, FF), F32 if precise else BF16),
        compiler_params=pltpu.CompilerParams(dimension_semantics=("parallel", "parallel"),
                                             vmem_limit_bytes=V7X_VMEM_LIMIT_BYTES),
        name="swiglu_up",
    )(x, wg, wu)


ROPE_TR = 512


def _rope_kernel(x_ref, cos_ref, sin_ref, o_ref):
    cos, sin = cos_ref[...], sin_ref[...]
    lane = lax.broadcasted_iota(jnp.int32, cos.shape, 1)
    first = lane < ROT_DIM // 2
    for h in range(x_ref.shape[1] // HEAD_DIM):
        x = x_ref[:, h * HEAD_DIM:(h + 1) * HEAD_DIM].astype(F32)
        partner = jnp.where(first, pltpu.roll(x, HEAD_DIM - ROT_DIM // 2, 1), pltpu.roll(x, ROT_DIM // 2, 1))
        o_ref[:, h * HEAD_DIM:(h + 1) * HEAD_DIM] = (x * cos + partner * sin).astype(o_ref.dtype)


def rope_tables(pos):
    half = ROT_DIM // 2
    inv = ROPE_THETA ** (-jnp.arange(half, dtype=F32) / half)
    ang = pos.astype(F32)[:, None] * inv[None, :]
    cos, sin = jnp.cos(ang), jnp.sin(ang)
    n = pos.shape[0]
    cos_t = jnp.concatenate([cos, cos, jnp.ones((n, HEAD_DIM - ROT_DIM), F32)], axis=1)
    sin_t = jnp.concatenate([-sin, sin, jnp.zeros((n, HEAD_DIM - ROT_DIM), F32)], axis=1)
    return cos_t, sin_t


def rope_rows(x, cos_t, sin_t, out_dtype):
    M, C = x.shape
    T = cos_t.shape[0]
    tr = min(ROPE_TR, T)
    assert T % tr == 0 and M % T == 0 and C % HEAD_DIM == 0
    nt = T // tr
    return pl.pallas_call(
        _rope_kernel,
        grid=(M // tr,),
        in_specs=[pl.BlockSpec((tr, C), lambda i: (i, 0)),
                  pl.BlockSpec((tr, HEAD_DIM), lambda i: (i % nt, 0)),
                  pl.BlockSpec((tr, HEAD_DIM), lambda i: (i % nt, 0))],
        out_specs=pl.BlockSpec((tr, C), lambda i: (i, 0)),
        out_shape=jax.ShapeDtypeStruct((M, C), out_dtype),
        compiler_params=pltpu.CompilerParams(dimension_semantics=("parallel",),
                                             vmem_limit_bytes=V7X_VMEM_LIMIT_BYTES),
        name="rope",
    )(x, cos_t, sin_t)


CMP_PAGES = 8
CHUNKS_PER_PAGE = PAGE_SIZE // CMP_STRIDE
CHUNK_ROWS = CMP_STRIDE * NSA_KV_GROUPS
CMP_PITCH = CHUNK_ROWS + 4


def _compress_kernel(pt_ref, *refs):
    page_refs, w_ref, o_ref, stage_ref = refs[:CMP_PAGES], refs[CMP_PAGES], refs[CMP_PAGES + 1], refs[CMP_PAGES + 2]
    for p, p_ref in enumerate(page_refs):
        for n in range(CHUNKS_PER_PAGE):
            stage_ref[p, pl.ds(n * CMP_PITCH, CHUNK_ROWS), :] = p_ref[0, 0, n * CHUNK_ROWS:(n + 1) * CHUNK_ROWS, :]
    slabs = []
    for g in range(NSA_KV_GROUPS):
        for p in range(CMP_PAGES):
            slabs.append(jnp.concatenate(
                [stage_ref[p, pl.ds(s * NSA_KV_GROUPS + g, CHUNKS_PER_PAGE, stride=CMP_PITCH), :]
                 for s in range(CMP_STRIDE)], axis=1))
    x = jnp.concatenate(slabs, axis=0).astype(BF16)
    y = jnp.dot(x, w_ref[...], preferred_element_type=F32)
    per_g = CMP_PAGES * CHUNKS_PER_PAGE
    for g in range(NSA_KV_GROUPS):
        o_ref[0, g] = y[g * per_g:(g + 1) * per_g]


def compress_first(pool, layer, page_flat, n_seq, w1):
    ppb = page_flat.shape[0] // n_seq
    assert ppb % CMP_PAGES == 0
    phi_h = w1.shape[-1]
    w = jnp.concatenate([w1[:CMP_STRIDE].reshape(CMP_STRIDE * HEAD_DIM, phi_h),
                         w1[CMP_STRIDE:].reshape(CMP_STRIDE * HEAD_DIM, phi_h)], axis=1).astype(BF16)
    steps = ppb // CMP_PAGES
    per_g = CMP_PAGES * CHUNKS_PER_PAGE

    def page_spec(i):
        return pl.BlockSpec((1, 1, PAGE_SIZE * NSA_KV_GROUPS, HEAD_DIM),
                            lambda b, j, pt: (layer, pt[b * ppb + j * CMP_PAGES + i], 0, 0))

    return pl.pallas_call(
        _compress_kernel,
        grid_spec=pltpu.PrefetchScalarGridSpec(
            num_scalar_prefetch=1, grid=(n_seq, steps),
            in_specs=[page_spec(i) for i in range(CMP_PAGES)]
                     + [pl.BlockSpec(w.shape, lambda b, j, pt: (0, 0))],
            out_specs=pl.BlockSpec((1, NSA_KV_GROUPS, per_g, 2 * phi_h), lambda b, j, pt: (b, 0, j, 0)),
            scratch_shapes=[pltpu.VMEM((CMP_PAGES, CHUNKS_PER_PAGE * CMP_PITCH, HEAD_DIM), F32)]),
        out_shape=jax.ShapeDtypeStruct((n_seq, NSA_KV_GROUPS, ppb * CHUNKS_PER_PAGE, 2 * phi_h), F32),
        compiler_params=pltpu.CompilerParams(dimension_semantics=("parallel", "arbitrary"),
                                             vmem_limit_bytes=V7X_VMEM_LIMIT_BYTES),
        name="compress_first",
    )(page_flat, *([pool] * CMP_PAGES), w)


def compress_finish(fs, w1, w2, pe):
    phi_h = w1.shape[-1]
    pe_bias = jnp.einsum('ld,lde->e', pe, w1)
    hid = jax.nn.gelu(fs[:, :, :-1, :phi_h] + fs[:, :, 1:, phi_h:] + pe_bias)
    b, G, n, _ = hid.shape
    assert (b * G) % 8 == 0
    return pmm(hid.reshape(b * G * n, phi_h), w2, tm=8 * n).reshape(b, G, n, w2.shape[-1])


NSA_TQ = 128
NSA_TK = 256
CB_PAD = 128


def _nt_dot(a, b):
    return lax.dot_general(a, b, (((1,), (1,)), ((), ())), preferred_element_type=F32)


def _split3_dot(x, w):
    hi = x.astype(BF16)
    r1 = x - hi.astype(F32)
    mid = r1.astype(BF16)
    lo = (r1 - mid.astype(F32)).astype(BF16)
    return (jnp.dot(hi, w, preferred_element_type=F32) + jnp.dot(mid, w, preferred_element_type=F32)
            + jnp.dot(lo, w, preferred_element_type=F32))


def _flash_tiles(q4, k_ref, v_ref, lo, hi, bias_fn):
    tq = NSA_TQ
    rows = q4.shape[0]

    def body(jt, carry):
        m, l, acc = carry
        k0 = pl.multiple_of(jt * NSA_TK, NSA_TK)
        kt = k_ref[0, pl.ds(k0, NSA_TK), :].astype(BF16)
        vt = v_ref[0, pl.ds(k0, NSA_TK), :].astype(BF16)
        bias = bias_fn(k0)
        sm = _nt_dot(q4, kt) * ATTN_SCALE + jnp.concatenate([bias] * NSA_REP, axis=0)
        m_new = jnp.maximum(m, jnp.max(sm, axis=-1, keepdims=True))
        alpha = jnp.exp(m - m_new)
        e = jnp.exp(sm - m_new)
        l = alpha * l + jnp.sum(e, axis=-1, keepdims=True)
        acc = alpha * acc + jnp.dot(e.astype(BF16), vt, preferred_element_type=F32)
        return m_new, l, acc

    init = (jnp.full((rows, 1), NEG_INF, F32), jnp.zeros((rows, 1), F32), jnp.zeros((rows, HEAD_DIM), F32))
    _, l, acc = lax.fori_loop(lo, hi, body, init)
    return acc / jnp.where(l > 0.0, l, 1.0)


def _nsa_prompt_kernel(q_ref, kc_ref, vc_ref, ks_ref, vs_ref, kw_ref, vw_ref, gate_ref, c2s_ref, o_ref,
                       *, n_cb, n_sb):
    tq = NSA_TQ
    q0 = pl.program_id(2) * tq
    qb = q_ref[0]
    q4 = jnp.concatenate([qb[:, r * HEAD_DIM:(r + 1) * HEAD_DIM] for r in range(NSA_REP)], axis=0)

    s = _nt_dot(q4, kc_ref[0, 0]) * ATTN_SCALE
    row = lax.broadcasted_iota(jnp.int32, s.shape, 0)
    tpos = q0 + (row & (tq - 1))
    n = lax.broadcasted_iota(jnp.int32, s.shape, 1)
    cmask = (n * CMP_STRIDE + (CMP_LEN - 1) <= tpos) & (n < n_cb)
    sm = jnp.where(cmask, s, NEG_INF)
    m_c = jnp.max(sm, axis=-1, keepdims=True)
    e = jnp.where(cmask, jnp.exp(sm - m_c), 0.0)
    l_c = jnp.sum(e, axis=-1, keepdims=True)
    p = e / jnp.where(l_c > 0.0, l_c, 1.0)
    o_c = jnp.dot(p.astype(BF16), vc_ref[0, 0], preferred_element_type=F32)

    psum = p[0:tq] + p[tq:2 * tq] + p[2 * tq:3 * tq] + p[3 * tq:4 * tq]
    imp_t = _split3_dot(psum, c2s_ref[...]).T
    imp_t = imp_t[:n_sb]
    j = lax.broadcasted_iota(jnp.int32, imp_t.shape, 0)
    t = q0 + lax.broadcasted_iota(jnp.int32, imp_t.shape, 1)
    cur = jnp.right_shift(t, SEL_SHIFT)
    valid = j * SEL_BLOCK <= t
    forced = (j == 0) | (j == cur) | (j == cur - 1)
    score = jnp.where(valid, imp_t + jnp.where(forced, FORCE_BONUS, 0.0), NEG_INF)
    rank = jnp.zeros(score.shape, jnp.int32)
    for i in range(n_sb):
        si = score[i:i + 1, :]
        beats = (si > score) | ((si == score) & (i < j))
        rank = rank + jnp.where(beats, 1, 0)
    sel_t = jnp.where((rank < N_SEL) & (score > 0.5 * NEG_INF), 1.0, 0.0)
    sel_t = jnp.concatenate([sel_t, jnp.zeros((CB_PAD - n_sb, tq), F32)], axis=0)
    sel = sel_t.T.astype(BF16)

    def slc_mask(k0):
        jb = lax.broadcasted_iota(jnp.int32, (CB_PAD, NSA_TK), 0)
        kc = k0 + lax.broadcasted_iota(jnp.int32, (CB_PAD, NSA_TK), 1)
        expand = jnp.where(jb == jnp.right_shift(kc, SEL_SHIFT), 1.0, 0.0).astype(BF16)
        selx = jnp.dot(sel, expand, preferred_element_type=F32)
        qpos = q0 + lax.broadcasted_iota(jnp.int32, (tq, NSA_TK), 0)
        kpos = k0 + lax.broadcasted_iota(jnp.int32, (tq, NSA_TK), 1)
        return jnp.where((selx > 0.5) & (kpos <= qpos), 0.0, NEG_INF)

    def win_mask(k0):
        qpos = q0 + lax.broadcasted_iota(jnp.int32, (tq, NSA_TK), 0)
        kpos = k0 + lax.broadcasted_iota(jnp.int32, (tq, NSA_TK), 1)
        d = qpos - kpos
        return jnp.where((d >= 0) & (d < WINDOW), 0.0, NEG_INF)

    hi = (q0 + tq - 1) // NSA_TK + 1
    o_s = _flash_tiles(q4, ks_ref, vs_ref, 0, hi, slc_mask)
    o_w = _flash_tiles(q4, kw_ref, vw_ref, jnp.maximum(q0 - (WINDOW - 1), 0) // NSA_TK, hi, win_mask)

    g = jax.nn.sigmoid(gate_ref[0, 0])
    for r in range(NSA_REP):
        rs = slice(r * tq, (r + 1) * tq)
        o = (g[:, r:r + 1] * o_c[rs] + g[:, NSA_REP + r:NSA_REP + r + 1] * o_s[rs]
             + g[:, 2 * NSA_REP + r:2 * NSA_REP + r + 1] * o_w[rs])
        o_ref[0, :, r * HEAD_DIM:(r + 1) * HEAD_DIM] = o.astype(o_ref.dtype)


def nsa_prompt_attention(q, kc, vc, k_slc, v_slc, k_win, v_win, g_nsa):
    b, T, _ = q.shape
    n_cb = kc.shape[2]
    n_sb = T // SEL_BLOCK
    assert T % NSA_TK == 0 and n_cb <= CB_PAD and n_sb <= CB_PAD and n_sb % 8 == 0
    G = NSA_KV_GROUPS

    def pack_c(x):
        return jnp.pad(x, ((0, 0), (0, 0), (0, CB_PAD - n_cb), (0, 0))).astype(BF16)

    gates = jnp.transpose(g_nsa.reshape(b, T, 3, G, NSA_REP), (0, 3, 1, 2, 4)).reshape(b, G, T, 3 * NSA_REP)
    i = np.arange(CB_PAD)[:, None] * CMP_STRIDE
    jj = np.arange(CB_PAD)[None, :] * SEL_BLOCK
    c2s = ((i < jj + SEL_BLOCK) & (i + CMP_LEN > jj) & (np.arange(CB_PAD)[:, None] < n_cb)
           & (np.arange(CB_PAD)[None, :] < n_sb))
    c2s = jnp.asarray(c2s, BF16)
    flat = lambda x: x.reshape(b, T, G * HEAD_DIM)
    kv_spec = pl.BlockSpec((1, T, HEAD_DIM), lambda bi, g, qi: (bi, 0, g))
    c_spec = pl.BlockSpec((1, 1, CB_PAD, HEAD_DIM), lambda bi, g, qi: (bi, g, 0, 0))
    return pl.pallas_call(
        functools.partial(_nsa_prompt_kernel, n_cb=n_cb, n_sb=n_sb),
        grid=(b, G, T // NSA_TQ),
        in_specs=[pl.BlockSpec((1, NSA_TQ, NSA_REP * HEAD_DIM), lambda bi, g, qi: (bi, qi, g)),
                  c_spec, c_spec, kv_spec, kv_spec, kv_spec, kv_spec,
                  pl.BlockSpec((1, 1, NSA_TQ, 3 * NSA_REP), lambda bi, g, qi: (bi, g, qi, 0)),
                  pl.BlockSpec((CB_PAD, CB_PAD), lambda bi, g, qi: (0, 0))],
        out_specs=pl.BlockSpec((1, NSA_TQ, NSA_REP * HEAD_DIM), lambda bi, g, qi: (bi, qi, g)),
        out_shape=jax.ShapeDtypeStruct((b, T, NSA_HEADS * HEAD_DIM), BF16),
        compiler_params=pltpu.CompilerParams(
            dimension_semantics=("parallel", "parallel", "arbitrary"),
            vmem_limit_bytes=V7X_VMEM_LIMIT_BYTES),
        name="nsa_prompt",
    )(q, pack_c(kc), pack_c(vc), flat(k_slc), flat(v_slc), flat(k_win), flat(v_win), gates, c2s)


SSD_GW = SSM_REP * SSM_HEAD_DIM
DT_PAD = 128


def _split3(x):
    hi = x.astype(BF16)
    r1 = x - hi.astype(F32)
    mid = r1.astype(BF16)
    lo = (r1 - mid.astype(F32)).astype(BF16)
    return hi, mid, lo


def _exact_left(w, x):
    hi, mid, lo = _split3(x)
    return (jnp.dot(w, hi, preferred_element_type=F32) + jnp.dot(w, mid, preferred_element_type=F32)
            + jnp.dot(w, lo, preferred_element_type=F32))


def _ssd_prompt_kernel(xbc_ref, z_ref, dt_ref, cw_ref, cb_ref, dtb_ref, aneg_ref, dsk_ref, nw_ref,
                       expand_ref, tril_ref, y_ref, hfin_ref, xprev_ref, ht_ref):
    c = pl.program_id(1)
    Q = SSD_CHUNK

    @pl.when(c == 0)
    def _():
        xprev_ref[...] = jnp.zeros_like(xprev_ref)
        ht_ref[...] = jnp.zeros_like(ht_ref)

    x = xbc_ref[0]
    xp = xprev_ref[...]
    rowi = lax.broadcasted_iota(jnp.int32, x.shape, 0)
    acc = jnp.broadcast_to(cb_ref[...], x.shape)
    for k in range(CONV_W):
        s = CONV_W - 1 - k
        if s == 0:
            xs_k = x
        else:
            xs_k = jnp.where(rowi >= s, pltpu.roll(x, s, 0), pltpu.roll(xp, s, 0))
        acc = acc + xs_k * cw_ref[k:k + 1, :]
    xprev_ref[...] = x
    xc = acc * jax.nn.sigmoid(acc)
    xs = xc[:, :D_INNER]

    lane = lax.broadcasted_iota(jnp.int32, (Q, DT_PAD), 1)
    dtin = dt_ref[0] + dtb_ref[...]
    dt = jnp.maximum(dtin, 0.0) + jnp.log1p(jnp.exp(-jnp.abs(dtin)))
    dt = jnp.where(lane < SSM_HEADS, dt, 0.0)
    acum = _exact_left(tril_ref[...], dt * aneg_ref[...])
    acum_t = acum.T
    eacum = jnp.exp(acum)
    decay = jnp.exp(acum[Q - 1:Q, :] - acum) * dt
    expand = expand_ref[...]
    dt_e = _split3_dot(dt, expand)
    eacum_e = _split3_dot(eacum, expand)
    decay_e = _split3_dot(decay, expand)
    dtx = (dt_e * xs).astype(BF16)
    dxs = (decay_e * xs).astype(BF16)

    li = lax.broadcasted_iota(jnp.int32, (Q, Q), 0)
    si = lax.broadcasted_iota(jnp.int32, (Q, Q), 1)
    causal = li >= si
    half = lax.broadcasted_iota(jnp.int32, (Q, 2 * SSM_HEAD_DIM), 1) < SSM_HEAD_DIM

    for g in range(SSM_GROUPS):
        gs = slice(g * SSD_GW, (g + 1) * SSD_GW)
        bg = xc[:, D_INNER + g * D_STATE:D_INNER + (g + 1) * D_STATE]
        cg = xc[:, D_INNER + SSM_GROUPS * D_STATE + g * D_STATE:D_INNER + SSM_GROUPS * D_STATE + (g + 1) * D_STATE]
        bg16, cg16 = bg.astype(BF16), cg.astype(BF16)
        cbm = _nt_dot(cg16, bg16)
        ht = ht_ref[g]
        y = jnp.dot(cg16, ht.astype(BF16), preferred_element_type=F32) * eacum_e[:, gs]
        pieces = []
        for i in range(SSM_REP // 2):
            ws = []
            for h in (g * SSM_REP + 2 * i, g * SSM_REP + 2 * i + 1):
                seg = acum[:, h:h + 1] - acum_t[h:h + 1, :]
                ws.append((cbm * jnp.exp(jnp.where(causal, seg, NEG_INF))).astype(BF16))
            dpair = dtx[:, g * SSD_GW + i * 2 * SSM_HEAD_DIM:g * SSD_GW + (i + 1) * 2 * SSM_HEAD_DIM]
            pieces.append(jnp.where(half, jnp.dot(ws[0], dpair, preferred_element_type=F32),
                                    jnp.dot(ws[1], dpair, preferred_element_type=F32)))
        y = jnp.concatenate(pieces, axis=1) + y + dsk_ref[:, gs] * xs[:, gs]
        zg = z_ref[0, :, gs]
        y = y * (zg * jax.nn.sigmoid(zg))
        y = y * lax.rsqrt(jnp.mean(y * y, axis=-1, keepdims=True) + RMS_EPS)
        y_ref[0, :, gs] = (y * nw_ref[:, gs]).astype(y_ref.dtype)
        ht_ref[g] = eacum_e[Q - 1:Q, gs] * ht + jnp.dot(bg.T.astype(BF16), dxs[:, gs], preferred_element_type=F32)

    @pl.when(c == pl.num_programs(1) - 1)
    def _():
        for g in range(SSM_GROUPS):
            hfin_ref[0, g * SSD_GW:(g + 1) * SSD_GW, :] = ht_ref[g].T


def ssd_prompt(xbc, z, dt_raw, conv_w, conv_b, dt_bias, a_log, d_skip, norm_w):
    b, T, _ = xbc.shape
    assert T % SSD_CHUNK == 0 and dt_raw.shape[-1] == DT_PAD
    pad = lambda v: jnp.pad(v.astype(F32), (0, DT_PAD - SSM_HEADS)).reshape(1, DT_PAD)
    head_of = np.arange(D_INNER) // SSM_HEAD_DIM
    expand = jnp.asarray(np.arange(DT_PAD)[:, None] == head_of[None, :], BF16)
    tril = jnp.asarray(np.tril(np.ones((SSD_CHUNK, SSD_CHUNK))), BF16)
    dsk = jnp.repeat(d_skip.astype(F32), SSM_HEAD_DIM).reshape(1, D_INNER)
    const = lambda shape: pl.BlockSpec(shape, lambda bi, c: (0,) * len(shape))
    y, hfin = pl.pallas_call(
        _ssd_prompt_kernel,
        grid=(b, T // SSD_CHUNK),
        in_specs=[pl.BlockSpec((1, SSD_CHUNK, CONV_DIM), lambda bi, c: (bi, c, 0)),
                  pl.BlockSpec((1, SSD_CHUNK, D_INNER), lambda bi, c: (bi, c, 0)),
                  pl.BlockSpec((1, SSD_CHUNK, DT_PAD), lambda bi, c: (bi, c, 0)),
                  const((CONV_W, CONV_DIM)), const((1, CONV_DIM)), const((1, DT_PAD)), const((1, DT_PAD)),
                  const((1, D_INNER)), const((1, D_INNER)), const((DT_PAD, D_INNER)),
                  const((SSD_CHUNK, SSD_CHUNK))],
        out_specs=[pl.BlockSpec((1, SSD_CHUNK, D_INNER), lambda bi, c: (bi, c, 0)),
                   pl.BlockSpec((1, D_INNER, D_STATE), lambda bi, c: (bi, 0, 0))],
        out_shape=[jax.ShapeDtypeStruct((b, T, D_INNER), BF16),
                   jax.ShapeDtypeStruct((b, D_INNER, D_STATE), F32)],
        scratch_shapes=[pltpu.VMEM((SSD_CHUNK, CONV_DIM), F32),
                        pltpu.VMEM((SSM_GROUPS, D_STATE, SSD_GW), F32)],
        compiler_params=pltpu.CompilerParams(dimension_semantics=("parallel", "arbitrary"),
                                             vmem_limit_bytes=V7X_VMEM_LIMIT_BYTES),
        name="ssd_prompt",
    )(xbc, z, dt_raw, conv_w.astype(F32), conv_b.astype(F32).reshape(1, CONV_DIM), pad(dt_bias),
      pad(-jnp.exp(a_log.astype(F32))), dsk, norm_w.astype(F32).reshape(1, D_INNER), expand, tril)
    return y, hfin.reshape(b, SSM_HEADS, SSM_HEAD_DIM, D_STATE)


MOE_TM = 1024
MOE_TN = 512
MOE_TK = 1024


def _moe_up_kernel(te_ref, nu_ref, x_ref, wg_ref, wu_ref, h_ref):
    @pl.when(pl.program_id(0) < nu_ref[0])
    def _():
        x = x_ref[...]
        g = jnp.dot(x, wg_ref[0].astype(BF16), preferred_element_type=F32)
        u = jnp.dot(x, wu_ref[0].astype(BF16), preferred_element_type=F32)
        h_ref[...] = (g * jax.nn.sigmoid(g) * u).astype(h_ref.dtype)


def _moe_down_kernel(te_ref, nu_ref, h_ref, wd_ref, gw_ref, y_ref):
    k = pl.program_id(1)

    @pl.when(pl.program_id(0) < nu_ref[0])
    def _():
        part = jnp.dot(h_ref[...], wd_ref[0].astype(BF16), preferred_element_type=F32)

        @pl.when(k == 0)
        def _():
            y_ref[...] = part

        @pl.when(k > 0)
        def _():
            y_ref[...] += part

        @pl.when(k == pl.num_programs(1) - 1)
        def _():
            y_ref[...] = y_ref[...] * gw_ref[...]


def moe_grouped(x_sorted, row_w, tile_expert, n_used, wg, wu, wd, tm):
    P, D = x_sorted.shape
    E, _, FF = wg.shape
    tn = min(MOE_TN, FF)
    tk = min(MOE_TK, FF)
    assert P % tm == 0 and FF % tn == 0 and FF % tk == 0
    nt, nj, nk = P // tm, FF // tn, FF // tk

    def row(i, nu):
        return jnp.minimum(i, nu[0] - 1)

    def col(i, j, nu, last):
        return jnp.where(i < nu[0], j, last)

    h = pl.pallas_call(
        _moe_up_kernel,
        grid_spec=pltpu.PrefetchScalarGridSpec(
            num_scalar_prefetch=2, grid=(nt, nj),
            in_specs=[pl.BlockSpec((tm, D), lambda i, j, te, nu: (row(i, nu), 0)),
                      pl.BlockSpec((1, D, tn), lambda i, j, te, nu: (te[row(i, nu)], 0, col(i, j, nu, nj - 1))),
                      pl.BlockSpec((1, D, tn), lambda i, j, te, nu: (te[row(i, nu)], 0, col(i, j, nu, nj - 1)))],
            out_specs=pl.BlockSpec((tm, tn), lambda i, j, te, nu: (row(i, nu), col(i, j, nu, nj - 1)))),
        out_shape=jax.ShapeDtypeStruct((P, FF), BF16),
        compiler_params=pltpu.CompilerParams(dimension_semantics=("arbitrary", "arbitrary"),
                                             vmem_limit_bytes=V7X_VMEM_LIMIT_BYTES),
        name="moe_up",
    )(tile_expert, n_used, x_sorted, wg, wu)
    return pl.pallas_call(
        _moe_down_kernel,
        grid_spec=pltpu.PrefetchScalarGridSpec(
            num_scalar_prefetch=2, grid=(nt, nk),
            in_specs=[pl.BlockSpec((tm, tk), lambda i, k, te, nu: (row(i, nu), col(i, k, nu, nk - 1))),
                      pl.BlockSpec((1, tk, D), lambda i, k, te, nu: (te[row(i, nu)], col(i, k, nu, nk - 1), 0)),
                      pl.BlockSpec((tm, 1), lambda i, k, te, nu: (row(i, nu), 0))],
            out_specs=pl.BlockSpec((tm, D), lambda i, k, te, nu: (row(i, nu), 0))),
        out_shape=jax.ShapeDtypeStruct((P, D), F32),
        compiler_params=pltpu.CompilerParams(dimension_semantics=("arbitrary", "arbitrary"),
                                             vmem_limit_bytes=V7X_VMEM_LIMIT_BYTES),
        name="moe_down",
    )(tile_expert, n_used, h, wd, row_w)


def moe_topk(u, w_router, wg, wu, wd, tm=MOE_TM):
    N, D = u.shape
    E = wg.shape[0]
    logits = jnp.dot(u, w_router, precision=lax.Precision.HIGHEST)
    top_l, top_i = lax.top_k(logits, TOP_K)
    top_w = jax.nn.softmax(top_l, axis=-1)
    flat_e = top_i.reshape(-1)
    onehot = (flat_e[:, None] == jnp.arange(E, dtype=flat_e.dtype)[None, :]).astype(jnp.int32)
    csum = jnp.cumsum(onehot, axis=0)
    rank = jnp.take_along_axis(csum, flat_e[:, None], axis=1)[:, 0] - 1
    tiles_per_e = (csum[-1] + tm - 1) // tm
    tile_end = jnp.cumsum(tiles_per_e)
    dest = (tile_end - tiles_per_e)[flat_e] * tm + rank
    nt = (TOP_K * N + E * (tm - 1)) // tm + 1
    P = nt * tm
    tile_expert = jnp.minimum(jnp.searchsorted(tile_end, jnp.arange(nt), side='right'), E - 1).astype(jnp.int32)
    n_used = tile_end[-1:].astype(jnp.int32)
    row_token = jnp.zeros((P,), jnp.int32).at[dest].set(jnp.arange(TOP_K * N, dtype=jnp.int32) // TOP_K)
    row_w = jnp.zeros((P, 1), F32).at[dest, 0].set(top_w.reshape(-1))
    x_sorted = u.astype(BF16).at[row_token].get(mode='promise_in_bounds')
    y_sorted = moe_grouped(x_sorted, row_w, tile_expert, n_used, wg, wu, wd, tm)
    dest_k = dest.reshape(N, TOP_K)
    y = y_sorted.at[dest_k[:, 0]].get(mode='promise_in_bounds')
    for k in range(1, TOP_K):
        y = y + y_sorted.at[dest_k[:, k]].get(mode='promise_in_bounds')
    return y


def rmsnorm(x, w):
    xf = x.astype(F32)
    y = xf * lax.rsqrt(jnp.mean(xf * xf, axis=-1, keepdims=True) + RMS_EPS)
    return (y * w.astype(F32)).astype(x.dtype)


def masked_softmax(s, mask):
    p = jax.nn.softmax(jnp.where(mask, s, NEG_INF), axis=-1)
    return p * mask


def rotary(x, pos):
    half = ROT_DIM // 2
    inv = ROPE_THETA ** (-jnp.arange(half, dtype=F32) / half)
    ang = pos.astype(F32)[:, None] * inv[None, :]
    ang = ang.reshape((1, pos.shape[0]) + (1,) * (x.ndim - 3) + (half,))
    cos, sin = jnp.cos(ang), jnp.sin(ang)
    xr = x[..., :ROT_DIM].astype(F32)
    x1, x2 = xr[..., :half], xr[..., half:]
    rot = jnp.concatenate([x1 * cos - x2 * sin, x2 * cos + x1 * sin], axis=-1)
    return jnp.concatenate([rot.astype(x.dtype), x[..., ROT_DIM:]], axis=-1)


def split_in(proj):
    cuts = np.cumsum(IN_SIZES)[:-1].tolist()
    return jnp.split(proj, cuts, axis=-1)


def ssd_chunked(x, dt, a, bm, cm, h0):
    b, T = x.shape[:2]
    q = SSD_CHUNK if T >= SSD_CHUNK else T
    nc = -(-T // q)
    pad = nc * q - T

    def chunks(v):
        v = jnp.pad(v.astype(F32), [(0, 0), (0, pad)] + [(0, 0)] * (v.ndim - 2))
        return jnp.moveaxis(v.reshape((b, nc, q) + v.shape[2:]), 1, 0)

    causal = jnp.tril(jnp.ones((q, q), bool))[None, :, :, None, None]

    def step(h, inp):
        xc, dtc, bc, cc = inp
        acum = jnp.cumsum(dtc * a, axis=1)
        seg = acum[:, :, None] - acum[:, None, :]
        lmat = jnp.exp(jnp.where(causal, seg, -jnp.inf))
        cb = jnp.einsum('blgn,bsgn->blsg', cc, bc)
        y_diag = jnp.einsum('blsg,blsgr,bsgrp->blgrp', cb, lmat, dtc[..., None] * xc)
        y_off = jnp.einsum('blgn,bgrpn->blgrp', cc, h) * jnp.exp(acum)[..., None]
        decay = jnp.exp(acum[:, -1:] - acum) * dtc
        h_new = jnp.exp(acum[:, -1])[..., None, None] * h + jnp.einsum('bsgn,bsgr,bsgrp->bgrpn', bc, decay, xc)
        return h_new, y_diag + y_off

    h_fin, ys = lax.scan(step, h0.astype(F32), (chunks(x), chunks(dt), chunks(bm), chunks(cm)))
    y = jnp.moveaxis(ys, 0, 1).reshape((b, nc * q) + x.shape[2:])[:, :T]
    return y, h_fin


def ssm_branch(z, xbc, dt_raw, conv0, h0, conv_w, conv_b, dt_bias, a_log, d_skip, norm_w):
    b, T, _ = xbc.shape
    xpad = jnp.concatenate([conv0.astype(xbc.dtype), xbc], axis=1)
    acc = conv_b
    for k in range(CONV_W):
        acc = acc + xpad[:, k:k + T] * conv_w[k]
    xbc_c = jax.nn.silu(acc)
    new_conv = xpad[:, T:]
    xs, bm, cm = jnp.split(xbc_c, [D_INNER, D_INNER + SSM_GROUPS * D_STATE], axis=-1)
    xs = xs.reshape(b, T, SSM_GROUPS, SSM_REP, SSM_HEAD_DIM)
    bm = bm.reshape(b, T, SSM_GROUPS, D_STATE)
    cm = cm.reshape(b, T, SSM_GROUPS, D_STATE)
    dt = jax.nn.softplus(dt_raw.astype(F32) + dt_bias.astype(F32)).reshape(b, T, SSM_GROUPS, SSM_REP)
    a = -jnp.exp(a_log.astype(F32)).reshape(SSM_GROUPS, SSM_REP)
    h0 = h0.reshape(b, SSM_GROUPS, SSM_REP, SSM_HEAD_DIM, D_STATE)
    y, h_fin = ssd_chunked(xs, dt, a, bm, cm, h0)
    y = y + d_skip.astype(F32).reshape(SSM_GROUPS, SSM_REP)[..., None] * xs.astype(F32)
    y = y.reshape(b, T, D_INNER) * jax.nn.silu(z.astype(F32))
    yg = y.reshape(b, T, SSM_GROUPS, D_INNER // SSM_GROUPS)
    yg = yg * lax.rsqrt(jnp.mean(yg * yg, axis=-1, keepdims=True) + RMS_EPS)
    y = yg.reshape(b, T, D_INNER) * norm_w.astype(F32)
    return y.astype(z.dtype), new_conv, h_fin.reshape(b, SSM_HEADS, SSM_HEAD_DIM, D_STATE)


def compress(rows, w1, w2, pe):
    b, L = rows.shape[:2]
    nf = L // CMP_STRIDE
    ch = rows[:, :nf * CMP_STRIDE].reshape(b, nf, CMP_STRIDE, NSA_KV_GROUPS, HEAD_DIM)
    first = jnp.einsum('bnsgd,sde->bnge', ch, w1[:CMP_STRIDE])
    second = jnp.einsum('bnsgd,sde->bnge', ch, w1[CMP_STRIDE:])
    pe_bias = jnp.einsum('ld,lde->e', pe, w1)
    hid = jax.nn.gelu(first[:, :-1] + second[:, 1:] + pe_bias)
    return jnp.einsum('bnge,ed->bngd', hid, w2)


def cmp_to_sel(n_cb, n_sb):
    i = jnp.arange(n_cb)[:, None] * CMP_STRIDE
    j = jnp.arange(n_sb)[None, :] * SEL_BLOCK
    return ((i < j + SEL_BLOCK) & (i + CMP_LEN > j)).astype(F32)


def cmp_attend(q, kc, vc, q_pos):
    n = kc.shape[1]
    s = jnp.einsum('btgrd,bngd->bgrtn', q, kc, preferred_element_type=F32) * ATTN_SCALE
    end = jnp.arange(n) * CMP_STRIDE + CMP_LEN - 1
    p = masked_softmax(s, end[None, :] <= q_pos[:, None])
    o = jnp.einsum('bgrtn,bngd->btgrd', p.astype(vc.dtype), vc)
    return o, p


def select_blocks(imp, q_pos):
    n_sb = imp.shape[-1]
    j = jnp.arange(n_sb)[None, :]
    cur = (q_pos // SEL_BLOCK)[:, None]
    valid = (j * SEL_BLOCK <= q_pos[:, None])[:, None, :]
    forced = ((j == 0) | (j == cur) | (j == cur - 1))[:, None, :]
    score = jnp.where(valid, imp + jnp.where(forced, FORCE_BONUS, 0.0), NEG_INF)
    top, idx = lax.top_k(score, min(N_SEL, n_sb))
    return idx, top > 0.5 * NEG_INF


def slc_attend(q, kb, vb, idx, sel_ok, q_pos):
    kpos = idx[..., None] * SEL_BLOCK + jnp.arange(SEL_BLOCK)
    mask = sel_ok[..., None] & (kpos <= q_pos[None, :, None, None, None])
    s = jnp.einsum('btgrd,btgkld->btgrkl', q, kb, preferred_element_type=F32) * ATTN_SCALE
    b, T, G, R, K, L = s.shape
    p = masked_softmax(s.reshape(b, T, G, R, K * L), mask.reshape(b, T, G, 1, K * L))
    return jnp.einsum('btgrkl,btgkld->btgrd', p.reshape(s.shape).astype(vb.dtype), vb)


def slc_prompt(q, k, v, idx, sel_ok, q_pos):
    b, T = q.shape[:2]
    nsb = T // SEL_BLOCK
    kblk = k.reshape(b, nsb, SEL_BLOCK, NSA_KV_GROUPS, HEAD_DIM)
    vblk = v.reshape(b, nsb, SEL_BLOCK, NSA_KV_GROUPS, HEAD_DIM)
    bi = jnp.arange(b)[:, None, None, None]
    gi = jnp.arange(NSA_KV_GROUPS)[None, None, :, None]
    nq = T // SLC_QCHUNK

    def chunked(a):
        return jnp.moveaxis(a.reshape((b, nq, SLC_QCHUNK) + a.shape[2:]), 1, 0)

    def body(args):
        qc, ic, okc, pc = args
        kb = kblk[bi, ic, :, gi, :]
        vb = vblk[bi, ic, :, gi, :]
        return slc_attend(qc, kb, vb, ic, okc, pc)

    out = lax.map(body, (chunked(q), chunked(idx), chunked(sel_ok), q_pos.reshape(nq, SLC_QCHUNK)))
    return jnp.moveaxis(out, 0, 1).reshape(q.shape)


def gather_selected(pool, layer, new_rows, idx, page_table, past):
    b, S = new_rows.shape[:2]
    bpp = PAGE_SIZE // SEL_BLOCK
    n_past_blk = past // SEL_BLOCK
    n_new_blk = -(-S // SEL_BLOCK)
    bi = jnp.arange(b)[:, None, None, None]
    gi = jnp.arange(NSA_KV_GROUPS)[None, None, :, None]
    jp = jnp.minimum(idx, n_past_blk - 1)
    phys = page_table[bi, jp // bpp]
    pool_blk = pool.reshape(pool.shape[0], pool.shape[1], bpp, SEL_BLOCK, NSA_KV_GROUPS, HEAD_DIM)
    from_past = pool_blk[layer, phys, jp % bpp, :, gi, :]
    new_blk = jnp.pad(new_rows, ((0, 0), (0, n_new_blk * SEL_BLOCK - S), (0, 0), (0, 0)))
    new_blk = new_blk.reshape(b, n_new_blk, SEL_BLOCK, NSA_KV_GROUPS, HEAD_DIM)
    jn = jnp.clip(idx - n_past_blk, 0, n_new_blk - 1)
    from_new = new_blk[bi, jn, :, gi, :]
    return jnp.where((idx < n_past_blk)[..., None, None], from_past, from_new.astype(from_past.dtype))


def window_attend(q, k, v, q_pos, k_pos):
    s = jnp.einsum('btgrd,bsgd->bgrts', q, k, preferred_element_type=F32) * ATTN_SCALE
    diff = q_pos[:, None] - k_pos[None, :]
    mask = (diff >= 0) & (diff < WINDOW) & (k_pos[None, :] >= 0)
    p = masked_softmax(s, mask)
    return jnp.einsum('bgrts,bsgd->btgrd', p.astype(v.dtype), v)


def win_prompt(q, k, v):
    b, T = q.shape[:2]
    nq = T // WIN_QBLOCK
    span = WIN_QBLOCK + WINDOW
    kp = jnp.pad(k, ((0, 0), (WINDOW, 0), (0, 0), (0, 0)))
    vp = jnp.pad(v, ((0, 0), (WINDOW, 0), (0, 0), (0, 0)))
    qb = jnp.moveaxis(q.reshape((b, nq, WIN_QBLOCK) + q.shape[2:]), 1, 0)

    def body(args):
        i, qc = args
        start = i * WIN_QBLOCK
        kc = lax.dynamic_slice_in_dim(kp, start, span, axis=1)
        vc = lax.dynamic_slice_in_dim(vp, start, span, axis=1)
        q_pos = start + jnp.arange(WIN_QBLOCK)
        k_pos = start - WINDOW + jnp.arange(span)
        return window_attend(qc, kc, vc, q_pos, k_pos)

    out = lax.map(body, (jnp.arange(nq), qb))
    return jnp.moveaxis(out, 0, 1).reshape(q.shape)


def combine_nsa(gate, o_cmp, o_slc, o_win, dtype):
    b, T = o_cmp.shape[:2]
    o = gate[:, :, 0] * o_cmp + gate[:, :, 1] * o_slc + gate[:, :, 2] * o_win
    return o.reshape(b, T, NSA_HEADS * HEAD_DIM).astype(dtype)


def nsa_prompt(q, kvs, g_nsa, phi):
    phi1_k, phi2_k, pe_k, phi1_v, phi2_v, pe_v = phi
    b, T, _ = q.shape
    cos_t, sin_t = rope_tables(jnp.arange(T, dtype=jnp.int32))
    heads = lambda x: x.reshape(b, T, NSA_KV_GROUPS, HEAD_DIM)
    rot = lambda x, dt: rope_rows(x.reshape(b * T, x.shape[-1]), cos_t, sin_t, dt).reshape(x.shape)
    q = rot(q, BF16)
    k_cmp, k_slc, k_win = heads(rot(kvs[0], F32)), heads(rot(kvs[2], F32)), heads(rot(kvs[4], F32))
    v_cmp, v_slc, v_win = heads(kvs[1]), heads(kvs[3]), heads(kvs[5])
    assert T % (PAGE_SIZE * CMP_PAGES) == 0
    pages = jnp.arange(b * T // PAGE_SIZE, dtype=jnp.int32)
    as_pool = lambda x: x.reshape(1, b * T // PAGE_SIZE, PAGE_SIZE * NSA_KV_GROUPS, HEAD_DIM)
    kc = compress_finish(compress_first(as_pool(k_cmp), 0, pages, b, phi1_k), phi1_k, phi2_k, pe_k)
    vc = compress_finish(compress_first(as_pool(v_cmp), 0, pages, b, phi1_v), phi1_v, phi2_v, pe_v)
    o = nsa_prompt_attention(q, kc, vc, k_slc, v_slc, k_win, v_win, g_nsa)
    wb = min(WINDOW, T)
    return o, (k_cmp, v_cmp, k_slc, v_slc, k_win[:, T - wb:], v_win[:, T - wb:])


def nsa_sample(q, kv, g_nsa, phi, layer, pool_k_cmp, pool_v_cmp, ck_slc, cv_slc, ck_win, cv_win, page_table):
    phi1_k, phi2_k, pe_k, phi1_v, phi2_v, pe_v = phi
    b, S = q.shape[:2]
    assert S < CMP_STRIDE
    gate = jax.nn.sigmoid(g_nsa.astype(F32)).reshape(b, S, 3, NSA_KV_GROUPS, NSA_REP, 1)
    past = page_table.shape[1] * PAGE_SIZE
    pos = past + jnp.arange(S, dtype=jnp.int32)
    q = rotary(q, pos)
    k_cmp, k_slc, k_win = rotary(kv[:, :, 0], pos), rotary(kv[:, :, 2], pos), rotary(kv[:, :, 4], pos)
    v_cmp, v_slc, v_win = kv[:, :, 1], kv[:, :, 3], kv[:, :, 5]

    def compress_past(pool, w1, w2, pe):
        pool = pool.reshape(pool.shape[:2] + (PAGE_SIZE * NSA_KV_GROUPS, HEAD_DIM))
        blocks = compress_finish(compress_first(pool, layer, page_table.reshape(-1), b, w1), w1, w2, pe)
        return jnp.transpose(blocks, (0, 2, 1, 3))

    kc = compress_past(pool_k_cmp, phi1_k, phi2_k, pe_k)
    vc = compress_past(pool_v_cmp, phi1_v, phi2_v, pe_v)
    wb = ck_win.shape[1]
    k_all = jnp.concatenate([ck_win.astype(k_win.dtype), k_win], axis=1)
    v_all = jnp.concatenate([cv_win.astype(v_win.dtype), v_win], axis=1)
    k_pos = past - wb + jnp.arange(wb + S)
    with jax.default_matmul_precision("highest"):
        o_cmp, p_cmp = cmp_attend(q, kc, vc, pos)
        imp = jnp.einsum('bgrtn,nj->btgj', p_cmp, cmp_to_sel(kc.shape[1], -(-(past + S) // SEL_BLOCK)))
        idx, sel_ok = select_blocks(imp, pos)
        kb = gather_selected(ck_slc, layer, k_slc, idx, page_table, past)
        vb = gather_selected(cv_slc, layer, v_slc, idx, page_table, past)
        o_slc = slc_attend(q, kb, vb, idx, sel_ok, pos)
        o_win = window_attend(q, k_all, v_all, pos, k_pos)
    o = combine_nsa(gate, o_cmp, o_slc, o_win, q.dtype)
    return o, (k_cmp, v_cmp, k_slc, v_slc, k_all[:, -wb:], v_all[:, -wb:])


def token_mixer(h, conv0, ssm0, nsa_fn, norm_w, w_in, conv_w, conv_b, dt_bias, a_log, d_skip,
                ssm_norm_w, w_br_ssm, w_br_nsa, w_out):
    b, T, _ = h.shape
    sample = conv0 is not None
    u = rmsnorm(h, norm_w)
    if not sample:
        u = u.astype(BF16)
    proj = functools.partial(mm, precise=sample)
    o = np.cumsum((0,) + IN_SIZES)
    z, xbc, q, g_nsa, g_merge = (proj(u, w_in[:, o[i]:o[i + 1]]) for i in (0, 1, 3, 5, 6))
    dt_pad = proj(u, w_in[:, o[2]:o[2] + DT_PAD])
    if not sample:
        y_ssm, ssm_new = ssd_prompt(xbc, z, dt_pad, conv_w, conv_b, dt_bias, a_log, d_skip, ssm_norm_w)
        conv_new = xbc[:, T - (CONV_W - 1):]
        kvs = [proj(u, w_in[:, o[4] + i * KV_WIDTH:o[4] + (i + 1) * KV_WIDTH]) for i in range(6)]
        o_nsa, nsa_state = nsa_fn(q, kvs, g_nsa)
    else:
        with jax.default_matmul_precision("highest"):
            y_ssm, conv_new, ssm_new = ssm_branch(z, xbc, dt_pad[..., :SSM_HEADS], conv0, ssm0, conv_w, conv_b,
                                                  dt_bias, a_log, d_skip, ssm_norm_w)
        q = q.reshape(b, T, NSA_KV_GROUPS, NSA_REP, HEAD_DIM)
        kv = proj(u, w_in[:, o[4]:o[5]]).reshape(b, T, 6, NSA_KV_GROUPS, HEAD_DIM)
        o_nsa, nsa_state = nsa_fn(q, kv, g_nsa)
    rows = lambda x: x.reshape(b * T, x.shape[-1])
    merged = gated_merge(rows(y_ssm), w_br_ssm, rows(o_nsa), w_br_nsa, rows(g_merge), precise=sample)
    out = pmm(merged, w_out, precise=sample).reshape(b, T, D_MODEL)
    return out, nsa_state + (ssm_new.astype(h.dtype), conv_new)


def swiglu(u, wg, wu, wd, precise=False):
    lead = u.shape[:-1]
    hid = swiglu_up(u.reshape(-1, u.shape[-1]), wg, wu, precise=precise)
    return pmm(hid, wd, precise=precise).reshape(lead + (wd.shape[-1],))


def channel_mixer(hp, hs, l, norm_w, w_gate, w_up, w_down, w_router, w_gate_e, w_up_e, w_down_e):
    up, us = rmsnorm(hp, norm_w), rmsnorm(hs, norm_w)
    i = l // 2
    if l % 2 == 0:
        return (swiglu(up.astype(BF16), w_gate[i], w_up[i], w_down[i]),
                swiglu(us, w_gate[i], w_up[i], w_down[i], precise=True))
    n_p = up.shape[0] * up.shape[1]
    u_all = jnp.concatenate([up.reshape(n_p, D_MODEL), us.reshape(-1, D_MODEL)], axis=0)
    y = moe_topk(u_all, w_router[i], w_gate_e[i], w_up_e[i], w_down_e[i])
    return y[:n_p].reshape(up.shape), y[n_p:].reshape(us.shape)


def kernel(x_prompt, x_sample, cache_k_cmp, cache_v_cmp, cache_k_slc, cache_v_slc, cache_k_win, cache_v_win, state_ssm, state_conv, page_table, norm_mix, w_in, conv_w, conv_b, dt_bias, a_log, d_skip, ssm_norm, phi1_k, phi2_k, pe_k, phi1_v, phi2_v, pe_v, w_br_ssm, w_br_nsa, w_out, norm_ffn, w_gate, w_up, w_down, w_router, w_gate_e, w_up_e, w_down_e, norm_final):
    hp, hs = x_prompt, x_sample
    bp = x_prompt.shape[0]
    p_states = [[] for _ in range(8)]
    s_states = [[] for _ in range(8)]
    for l in range(DEPTH):
        mw = (norm_mix[l], w_in[l], conv_w[l], conv_b[l], dt_bias[l], a_log[l], d_skip[l],
              ssm_norm[l], w_br_ssm[l], w_br_nsa[l], w_out[l])
        phi = (phi1_k[l], phi2_k[l], pe_k[l], phi1_v[l], phi2_v[l], pe_v[l])
        ffn = (w_gate, w_up, w_down, w_router, w_gate_e, w_up_e, w_down_e)
        out, st = token_mixer(hp, None, None, functools.partial(nsa_prompt, phi=phi), *mw)
        hp = hp + out
        for i in range(8):
            p_states[i].append(st[i])
        nsa_fn = functools.partial(nsa_sample, phi=phi, layer=l, pool_k_cmp=cache_k_cmp, pool_v_cmp=cache_v_cmp,
                                   ck_slc=cache_k_slc, cv_slc=cache_v_slc,
                                   ck_win=cache_k_win[l], cv_win=cache_v_win[l], page_table=page_table)
        out, st = token_mixer(hs, state_conv[l], state_ssm[l], nsa_fn, *mw)
        hs = hs + out
        for i in range(8):
            s_states[i].append(st[i])
        fp, fs = channel_mixer(hp, hs, l, norm_ffn[l], *ffn)
        hp, hs = hp + fp, hs + fs
    y_prompt = rmsnorm(hp, norm_final)
    y_sample = rmsnorm(hs, norm_final)
    p_out = [jnp.stack(a) for a in p_states]
    s_out = [jnp.stack(a) for a in s_states]
    return (y_prompt, y_sample, *p_out, *s_out)
```

```python
import functools
import math

import jax
import jax.numpy as jnp
import numpy as np
from jax import lax
from jax.experimental import pallas as pl
from jax.experimental.pallas import tpu as pltpu

F32 = jnp.float32
BF16 = jnp.bfloat16

D_MODEL = 2048
DEPTH = 2
PAGE_SIZE = 128
D_INNER = 4096
SSM_HEAD_DIM = 64
SSM_HEADS = 64
SSM_GROUPS = 8
SSM_REP = 8
D_STATE = 128
CONV_W = 4
CONV_DIM = D_INNER + 2 * SSM_GROUPS * D_STATE
SSD_CHUNK = 128
NSA_HEADS = 16
NSA_KV_GROUPS = 4
NSA_REP = 4
HEAD_DIM = 128
KV_WIDTH = NSA_KV_GROUPS * HEAD_DIM
ROT_DIM = HEAD_DIM // 4
ROPE_THETA = 500000.0
ATTN_SCALE = HEAD_DIM ** -0.5
CMP_STRIDE = 16
CMP_LEN = 32
SEL_BLOCK = 64
SEL_SHIFT = 6
N_SEL = 16
WINDOW = 512
WIN_QBLOCK = 128
SLC_QCHUNK = 16
FORCE_BONUS = 1.0e4
NEG_INF = -1.0e30
N_EXPERTS = 8
TOP_K = 2
RMS_EPS = 1e-6
IN_SIZES = (D_INNER, CONV_DIM, SSM_HEADS, NSA_HEADS * HEAD_DIM, 6 * KV_WIDTH, 3 * NSA_HEADS, 2 * D_MODEL)

V7X_VMEM_LIMIT_BYTES = 56 * 1024 * 1024
LANE = 128


def _dot(x, w, precise):
    if precise:
        return jnp.dot(x.astype(F32), w.astype(F32), preferred_element_type=F32,
                       precision=lax.Precision.HIGHEST)
    return jnp.dot(x.astype(BF16), w.astype(BF16), preferred_element_type=F32)


def _mm_kernel(x_ref, w_ref, o_ref, acc_ref, *, precise):
    k = pl.program_id(2)

    @pl.when(k == 0)
    def _():
        acc_ref[...] = jnp.zeros_like(acc_ref)

    acc_ref[...] += _dot(x_ref[...], w_ref[...], precise)

    @pl.when(k == pl.num_programs(2) - 1)
    def _():
        o_ref[...] = acc_ref[...]


def _pick_tk(K, cap=2048):
    best = None
    for t in range(LANE, min(K, cap) + 1, LANE):
        if K % t == 0:
            best = t
    return best if best is not None else K


def pmm(x, w, tm=1024, tn=512, precise=False):
    M, K = x.shape
    K2, N = w.shape
    assert K == K2
    tm = min(tm, M)
    tn = min(tn, N)
    tk = _pick_tk(K)
    assert M % tm == 0
    grid = (M // tm, pl.cdiv(N, tn), K // tk)
    return pl.pallas_call(
        functools.partial(_mm_kernel, precise=precise),
        grid=grid,
        in_specs=[pl.BlockSpec((tm, tk), lambda i, j, k: (i, k)),
                  pl.BlockSpec((tk, tn), lambda i, j, k: (k, j))],
        out_specs=pl.BlockSpec((tm, tn), lambda i, j, k: (i, j)),
        out_shape=jax.ShapeDtypeStruct((M, N), F32),
        scratch_shapes=[pltpu.VMEM((tm, tn), F32)],
        compiler_params=pltpu.CompilerParams(
            dimension_semantics=("parallel", "parallel", "arbitrary"),
            vmem_limit_bytes=V7X_VMEM_LIMIT_BYTES),
        name="pmm",
    )(x, w)


def mm(x, w, precise=False):
    lead = x.shape[:-1]
    return pmm(x.reshape(-1, x.shape[-1]), w, precise=precise).reshape(lead + (w.shape[-1],))


def _merge_kernel(a_ref, wa_ref, b_ref, wb_ref, ga_ref, gb_ref, o_ref, *, precise):
    ya = _dot(a_ref[...], wa_ref[...], precise)
    yb = _dot(b_ref[...], wb_ref[...], precise)
    o_ref[...] = (jax.nn.sigmoid(ga_ref[...]) * ya + jax.nn.sigmoid(gb_ref[...]) * yb).astype(o_ref.dtype)


def gated_merge(a, wa, b, wb, g_merge, tm=1024, tn=256, precise=False):
    M, Ka = a.shape
    Kb = b.shape[1]
    N = wa.shape[1]
    tm, tn = min(tm, M), min(tn, N)
    assert M % tm == 0 and N % tn == 0 and g_merge.shape == (M, 2 * N)
    nj = N // tn
    return pl.pallas_call(
        functools.partial(_merge_kernel, precise=precise),
        grid=(M // tm, nj),
        in_specs=[pl.BlockSpec((tm, Ka), lambda i, j: (i, 0)),
                  pl.BlockSpec((Ka, tn), lambda i, j: (0, j)),
                  pl.BlockSpec((tm, Kb), lambda i, j: (i, 0)),
                  pl.BlockSpec((Kb, tn), lambda i, j: (0, j)),
                  pl.BlockSpec((tm, tn), lambda i, j: (i, j)),
                  pl.BlockSpec((tm, tn), lambda i, j: (i, j + nj))],
        out_specs=pl.BlockSpec((tm, tn), lambda i, j: (i, j)),
        out_shape=jax.ShapeDtypeStruct((M, N), F32 if precise else BF16),
        compiler_params=pltpu.CompilerParams(dimension_semantics=("parallel", "parallel"),
                                             vmem_limit_bytes=V7X_VMEM_LIMIT_BYTES),
        name="gated_merge",
    )(a, wa, b, wb, g_merge, g_merge)


def _swiglu_up_kernel(x_ref, wg_ref, wu_ref, h_ref, *, precise):
    x = x_ref[...]
    g = _dot(x, wg_ref[...], precise)
    u = _dot(x, wu_ref[...], precise)
    h_ref[...] = (g * jax.nn.sigmoid(g) * u).astype(h_ref.dtype)


def swiglu_up(x, wg, wu, tm=1024, tn=512, precise=False):
    M, D = x.shape
    FF = wg.shape[1]
    tm, tn = min(tm, M), min(tn, FF)
    assert M % tm == 0 and FF % tn == 0
    return pl.pallas_call(
        functools.partial(_swiglu_up_kernel, precise=precise),
        grid=(M // tm, FF // tn),
        in_specs=[pl.BlockSpec((tm, D), lambda i, j: (i, 0)),
                  pl.BlockSpec((D, tn), lambda i, j: (0, j)),
                  pl.BlockSpec((D, tn), lambda i, j: (0, j))],
        out_specs=pl.BlockSpec((tm, tn), lambda i, j: (i, j)),
        out_shape=jax.ShapeDtypeStruct((M, FF), F32 if precise else BF16),
        compiler_params=pltpu.CompilerParams(dimension_semantics=("parallel", "parallel"),
                                             vmem_limit_bytes=V7X_VMEM_LIMIT_BYTES),
        name="swiglu_up",
    )(x, wg, wu)


ROPE_TR = 512


def _rope_kernel(x_ref, cos_ref, sin_ref, o_ref):
    cos, sin = cos_ref[...], sin_ref[...]
    lane = lax.broadcasted_iota(jnp.int32, cos.shape, 1)
    first = lane < ROT_DIM // 2
    for h in range(x_ref.shape[1] // HEAD_DIM):
        x = x_ref[:, h * HEAD_DIM:(h + 1) * HEAD_DIM].astype(F32)
        partner = jnp.where(first, pltpu.roll(x, HEAD_DIM - ROT_DIM // 2, 1), pltpu.roll(x, ROT_DIM // 2, 1))
        o_ref[:, h * HEAD_DIM:(h + 1) * HEAD_DIM] = (x * cos + partner * sin).astype(o_ref.dtype)


def rope_tables(pos):
    half = ROT_DIM // 2
    inv = ROPE_THETA ** (-jnp.arange(half, dtype=F32) / half)
    ang = pos.astype(F32)[:, None] * inv[None, :]
    cos, sin = jnp.cos(ang), jnp.sin(ang)
    n = pos.shape[0]
    cos_t = jnp.concatenate([cos, cos, jnp.ones((n, HEAD_DIM - ROT_DIM), F32)], axis=1)
    sin_t = jnp.concatenate([-sin, sin, jnp.zeros((n, HEAD_DIM - ROT_DIM), F32)], axis=1)
    return cos_t, sin_t


def rope_rows(x, cos_t, sin_t, out_dtype):
    M, C = x.shape
    T = cos_t.shape[0]
    tr = min(ROPE_TR, T)
    assert T % tr == 0 and M % T == 0 and C % HEAD_DIM == 0
    nt = T // tr
    return pl.pallas_call(
        _rope_kernel,
        grid=(M // tr,),
        in_specs=[pl.BlockSpec((tr, C), lambda i: (i, 0)),
                  pl.BlockSpec((tr, HEAD_DIM), lambda i: (i % nt, 0)),
                  pl.BlockSpec((tr, HEAD_DIM), lambda i: (i % nt, 0))],
        out_specs=pl.BlockSpec((tr, C), lambda i: (i, 0)),
        out_shape=jax.ShapeDtypeStruct((M, C), out_dtype),
        compiler_params=pltpu.CompilerParams(dimension_semantics=("parallel",),
                                             vmem_limit_bytes=V7X_VMEM_LIMIT_BYTES),
        name="rope",
    )(x, cos_t, sin_t)


CMP_PAGES = 8
CHUNKS_PER_PAGE = PAGE_SIZE // CMP_STRIDE
CHUNK_ROWS = CMP_STRIDE * NSA_KV_GROUPS
CMP_PITCH = CHUNK_ROWS + 4


def _compress_kernel(pt_ref, *refs):
    page_refs, w_ref, o_ref, stage_ref = refs[:CMP_PAGES], refs[CMP_PAGES], refs[CMP_PAGES + 1], refs[CMP_PAGES + 2]
    for p, p_ref in enumerate(page_refs):
        for n in range(CHUNKS_PER_PAGE):
            stage_ref[p, pl.ds(n * CMP_PITCH, CHUNK_ROWS), :] = p_ref[0, 0, n * CHUNK_ROWS:(n + 1) * CHUNK_ROWS, :]
    slabs = []
    for g in range(NSA_KV_GROUPS):
        for p in range(CMP_PAGES):
            slabs.append(jnp.concatenate(
                [stage_ref[p, pl.ds(s * NSA_KV_GROUPS + g, CHUNKS_PER_PAGE, stride=CMP_PITCH), :]
                 for s in range(CMP_STRIDE)], axis=1))
    x = jnp.concatenate(slabs, axis=0).astype(BF16)
    y = jnp.dot(x, w_ref[...], preferred_element_type=F32)
    per_g = CMP_PAGES * CHUNKS_PER_PAGE
    for g in range(NSA_KV_GROUPS):
        o_ref[0, g] = y[g * per_g:(g + 1) * per_g]


def compress_first(pool, layer, page_flat, n_seq, w1):
    ppb = page_flat.shape[0] // n_seq
    assert ppb % CMP_PAGES == 0
    phi_h = w1.shape[-1]
    w = jnp.concatenate([w1[:CMP_STRIDE].reshape(CMP_STRIDE * HEAD_DIM, phi_h),
                         w1[CMP_STRIDE:].reshape(CMP_STRIDE * HEAD_DIM, phi_h)], axis=1).astype(BF16)
    steps = ppb // CMP_PAGES
    per_g = CMP_PAGES * CHUNKS_PER_PAGE

    def page_spec(i):
        return pl.BlockSpec((1, 1, PAGE_SIZE * NSA_KV_GROUPS, HEAD_DIM),
                            lambda b, j, pt: (layer, pt[b * ppb + j * CMP_PAGES + i], 0, 0))

    return pl.pallas_call(
        _compress_kernel,
        grid_spec=pltpu.PrefetchScalarGridSpec(
            num_scalar_prefetch=1, grid=(n_seq, steps),
            in_specs=[page_spec(i) for i in range(CMP_PAGES)]
                     + [pl.BlockSpec(w.shape, lambda b, j, pt: (0, 0))],
            out_specs=pl.BlockSpec((1, NSA_KV_GROUPS, per_g, 2 * phi_h), lambda b, j, pt: (b, 0, j, 0)),
            scratch_shapes=[pltpu.VMEM((CMP_PAGES, CHUNKS_PER_PAGE * CMP_PITCH, HEAD_DIM), F32)]),
        out_shape=jax.ShapeDtypeStruct((n_seq, NSA_KV_GROUPS, ppb * CHUNKS_PER_PAGE, 2 * phi_h), F32),
        compiler_params=pltpu.CompilerParams(dimension_semantics=("parallel", "arbitrary"),
                                             vmem_limit_bytes=V7X_VMEM_LIMIT_BYTES),
        name="compress_first",
    )(page_flat, *([pool] * CMP_PAGES), w)


def compress_finish(fs, w1, w2, pe):
    phi_h = w1.shape[-1]
    pe_bias = jnp.einsum('ld,lde->e', pe, w1)
    hid = jax.nn.gelu(fs[:, :, :-1, :phi_h] + fs[:, :, 1:, phi_h:] + pe_bias)
    b, G, n, _ = hid.shape
    assert (b * G) % 8 == 0
    return pmm(hid.reshape(b * G * n, phi_h), w2, tm=8 * n).reshape(b, G, n, w2.shape[-1])


NSA_TQ = 128
NSA_TK = 256
CB_PAD = 128


def _nt_dot(a, b):
    return lax.dot_general(a, b, (((1,), (1,)), ((), ())), preferred_element_type=F32)


def _split3_dot(x, w):
    hi = x.astype(BF16)
    r1 = x - hi.astype(F32)
    mid = r1.astype(BF16)
    lo = (r1 - mid.astype(F32)).astype(BF16)
    return (jnp.dot(hi, w, preferred_element_type=F32) + jnp.dot(mid, w, preferred_element_type=F32)
            + jnp.dot(lo, w, preferred_element_type=F32))


def _flash_tiles(q4t, k_ref, vt_ref, lo, hi, bias_fn):
    cols = q4t.shape[1]
    last_k0 = k_ref.shape[1] - NSA_TK

    def scores(jt):
        k0 = jt * NSA_TK
        ka = pl.multiple_of(jnp.minimum(k0, last_k0), NSA_TK)
        kt = k_ref[0, pl.ds(ka, NSA_TK), :].astype(BF16)
        vtt = vt_ref[0, :, pl.ds(ka, NSA_TK)].astype(BF16)
        sm = (jnp.dot(kt, q4t, preferred_element_type=F32) * ATTN_SCALE
              + jnp.concatenate([bias_fn(k0)] * NSA_REP, axis=1))
        return sm, vtt

    def update(carry, sm, vtt):
        m, l, acc = carry
        m_new = jnp.maximum(m, jnp.max(sm, axis=0, keepdims=True))
        alpha = jnp.exp(m - m_new)
        e = jnp.exp(sm - m_new)
        l = alpha * l + jnp.sum(e, axis=0, keepdims=True)
        acc = alpha * acc + jnp.dot(vtt, e.astype(BF16), preferred_element_type=F32)
        return m_new, l, acc

    def body(i, carry):
        first, second = scores(lo + 2 * i), scores(lo + 2 * i + 1)
        return update(update(carry, *first), *second)

    init = (jnp.full((1, cols), NEG_INF, F32), jnp.zeros((1, cols), F32), jnp.zeros((HEAD_DIM, cols), F32))
    _, l, acc = lax.fori_loop(0, (hi - lo + 1) // 2, body, init)
    return acc / jnp.where(l > 0.0, l, 1.0)


def _nsa_prompt_kernel(q_ref, kc_ref, vc_ref, ks_ref, vst_ref, kw_ref, vwt_ref, gate_ref, c2s_ref, o_ref,
                       *, n_cb, n_sb):
    tq = NSA_TQ
    q0 = pl.program_id(2) * tq
    qb = q_ref[0]
    q4 = jnp.concatenate([qb[:, r * HEAD_DIM:(r + 1) * HEAD_DIM] for r in range(NSA_REP)], axis=0)

    s = _nt_dot(q4, kc_ref[0, 0]) * ATTN_SCALE
    row = lax.broadcasted_iota(jnp.int32, s.shape, 0)
    tpos = q0 + (row & (tq - 1))
    n = lax.broadcasted_iota(jnp.int32, s.shape, 1)
    cmask = (n * CMP_STRIDE + (CMP_LEN - 1) <= tpos) & (n < n_cb)
    sm = jnp.where(cmask, s, NEG_INF)
    m_c = jnp.max(sm, axis=-1, keepdims=True)
    e = jnp.where(cmask, jnp.exp(sm - m_c), 0.0)
    l_c = jnp.sum(e, axis=-1, keepdims=True)
    p = e / jnp.where(l_c > 0.0, l_c, 1.0)
    o_c = jnp.dot(p.astype(BF16), vc_ref[0, 0], preferred_element_type=F32)

    psum = p[0:tq] + p[tq:2 * tq] + p[2 * tq:3 * tq] + p[3 * tq:4 * tq]
    imp_t = _split3_dot(psum, c2s_ref[...]).T
    imp_t = imp_t[:n_sb]
    j = lax.broadcasted_iota(jnp.int32, imp_t.shape, 0)
    t = q0 + lax.broadcasted_iota(jnp.int32, imp_t.shape, 1)
    cur = jnp.right_shift(t, SEL_SHIFT)
    valid = j * SEL_BLOCK <= t
    forced = (j == 0) | (j == cur) | (j == cur - 1)
    score = jnp.where(valid, imp_t + jnp.where(forced, FORCE_BONUS, 0.0), NEG_INF)
    rank = jnp.zeros(score.shape, jnp.int32)
    for i in range(n_sb):
        si = score[i:i + 1, :]
        beats = (si > score) | ((si == score) & (i < j))
        rank = rank + jnp.where(beats, 1, 0)
    sel_t = jnp.where((rank < N_SEL) & (score > 0.5 * NEG_INF), 1.0, 0.0)
    sel_t = jnp.concatenate([sel_t, jnp.zeros((CB_PAD - n_sb, tq), F32)], axis=0).astype(BF16)

    def slc_mask(k0):
        kc = k0 + lax.broadcasted_iota(jnp.int32, (NSA_TK, CB_PAD), 0)
        jb = lax.broadcasted_iota(jnp.int32, (NSA_TK, CB_PAD), 1)
        expand = jnp.where(jb == jnp.right_shift(kc, SEL_SHIFT), 1.0, 0.0).astype(BF16)
        selx = jnp.dot(expand, sel_t, preferred_element_type=F32)
        kpos = k0 + lax.broadcasted_iota(jnp.int32, (NSA_TK, tq), 0)
        qpos = q0 + lax.broadcasted_iota(jnp.int32, (NSA_TK, tq), 1)
        return jnp.where((selx > 0.5) & (kpos <= qpos), 0.0, NEG_INF)

    def win_mask(k0):
        kpos = k0 + lax.broadcasted_iota(jnp.int32, (NSA_TK, tq), 0)
        qpos = q0 + lax.broadcasted_iota(jnp.int32, (NSA_TK, tq), 1)
        d = qpos - kpos
        return jnp.where((d >= 0) & (d < WINDOW), 0.0, NEG_INF)

    q4t = q4.astype(F32).T.astype(BF16)
    hi = (q0 + tq - 1) // NSA_TK + 1
    o_s = _flash_tiles(q4t, ks_ref, vst_ref, 0, hi, slc_mask).T
    o_w = _flash_tiles(q4t, kw_ref, vwt_ref, jnp.maximum(q0 - (WINDOW - 1), 0) // NSA_TK, hi, win_mask).T

    g = jax.nn.sigmoid(gate_ref[0, 0])
    for r in range(NSA_REP):
        rs = slice(r * tq, (r + 1) * tq)
        o = (g[:, r:r + 1] * o_c[rs] + g[:, NSA_REP + r:NSA_REP + r + 1] * o_s[rs]
             + g[:, 2 * NSA_REP + r:2 * NSA_REP + r + 1] * o_w[rs])
        o_ref[0, :, r * HEAD_DIM:(r + 1) * HEAD_DIM] = o.astype(o_ref.dtype)


def nsa_prompt_attention(q, kc, vc, k_slc, v_slc, k_win, v_win, g_nsa):
    b, T, _ = q.shape
    n_cb = kc.shape[2]
    n_sb = T // SEL_BLOCK
    assert T % NSA_TK == 0 and n_cb <= CB_PAD and n_sb <= CB_PAD and n_sb % 8 == 0
    G = NSA_KV_GROUPS

    def pack_c(x):
        return jnp.pad(x, ((0, 0), (0, 0), (0, CB_PAD - n_cb), (0, 0))).astype(BF16)

    gates = jnp.transpose(g_nsa.reshape(b, T, 3, G, NSA_REP), (0, 3, 1, 2, 4)).reshape(b, G, T, 3 * NSA_REP)
    i = np.arange(CB_PAD)[:, None] * CMP_STRIDE
    jj = np.arange(CB_PAD)[None, :] * SEL_BLOCK
    c2s = ((i < jj + SEL_BLOCK) & (i + CMP_LEN > jj) & (np.arange(CB_PAD)[:, None] < n_cb)
           & (np.arange(CB_PAD)[None, :] < n_sb))
    c2s = jnp.asarray(c2s, BF16)
    flat = lambda x: x.reshape(b, T, G * HEAD_DIM)
    flat_t = lambda x: jnp.transpose(x, (0, 2, 3, 1)).reshape(b, G * HEAD_DIM, T)
    kv_spec = pl.BlockSpec((1, T, HEAD_DIM), lambda bi, g, qi: (bi, 0, g))
    vt_spec = pl.BlockSpec((1, HEAD_DIM, T), lambda bi, g, qi: (bi, g, 0))
    c_spec = pl.BlockSpec((1, 1, CB_PAD, HEAD_DIM), lambda bi, g, qi: (bi, g, 0, 0))
    return pl.pallas_call(
        functools.partial(_nsa_prompt_kernel, n_cb=n_cb, n_sb=n_sb),
        grid=(b, G, T // NSA_TQ),
        in_specs=[pl.BlockSpec((1, NSA_TQ, NSA_REP * HEAD_DIM), lambda bi, g, qi: (bi, qi, g)),
                  c_spec, c_spec, kv_spec, vt_spec, kv_spec, vt_spec,
                  pl.BlockSpec((1, 1, NSA_TQ, 3 * NSA_REP), lambda bi, g, qi: (bi, g, qi, 0)),
                  pl.BlockSpec((CB_PAD, CB_PAD), lambda bi, g, qi: (0, 0))],
        out_specs=pl.BlockSpec((1, NSA_TQ, NSA_REP * HEAD_DIM), lambda bi, g, qi: (bi, qi, g)),
        out_shape=jax.ShapeDtypeStruct((b, T, NSA_HEADS * HEAD_DIM), BF16),
        compiler_params=pltpu.CompilerParams(
            dimension_semantics=("parallel", "parallel", "arbitrary"),
            vmem_limit_bytes=V7X_VMEM_LIMIT_BYTES),
        name="nsa_prompt",
    )(q, pack_c(kc), pack_c(vc), flat(k_slc), flat_t(v_slc), flat(k_win), flat_t(v_win), gates, c2s)


SSD_GW = SSM_REP * SSM_HEAD_DIM
DT_PAD = 128


def _split3(x):
    hi = x.astype(BF16)
    r1 = x - hi.astype(F32)
    mid = r1.astype(BF16)
    lo = (r1 - mid.astype(F32)).astype(BF16)
    return hi, mid, lo


def _exact_left(w, x):
    hi, mid, lo = _split3(x)
    return (jnp.dot(w, hi, preferred_element_type=F32) + jnp.dot(w, mid, preferred_element_type=F32)
            + jnp.dot(w, lo, preferred_element_type=F32))


def _ssd_prompt_kernel(xbc_ref, z_ref, dt_ref, cw_ref, cb_ref, dtb_ref, aneg_ref, dsk_ref, nw_ref,
                       expand_ref, tril_ref, y_ref, hfin_ref, xprev_ref, ht_ref):
    c = pl.program_id(1)
    Q = SSD_CHUNK

    @pl.when(c == 0)
    def _():
        xprev_ref[...] = jnp.zeros_like(xprev_ref)
        ht_ref[...] = jnp.zeros_like(ht_ref)

    x = xbc_ref[0]
    xp = xprev_ref[...]
    rowi = lax.broadcasted_iota(jnp.int32, x.shape, 0)
    acc = jnp.broadcast_to(cb_ref[...], x.shape)
    for k in range(CONV_W):
        s = CONV_W - 1 - k
        if s == 0:
            xs_k = x
        else:
            xs_k = jnp.where(rowi >= s, pltpu.roll(x, s, 0), pltpu.roll(xp, s, 0))
        acc = acc + xs_k * cw_ref[k:k + 1, :]
    xprev_ref[...] = x
    xc = acc * jax.nn.sigmoid(acc)
    xs = xc[:, :D_INNER]

    lane = lax.broadcasted_iota(jnp.int32, (Q, DT_PAD), 1)
    dtin = dt_ref[0] + dtb_ref[...]
    dt = jnp.maximum(dtin, 0.0) + jnp.log1p(jnp.exp(-jnp.abs(dtin)))
    dt = jnp.where(lane < SSM_HEADS, dt, 0.0)
    acum = _exact_left(tril_ref[...], dt * aneg_ref[...])
    acum_t = acum.T
    eacum = jnp.exp(acum)
    decay = jnp.exp(acum[Q - 1:Q, :] - acum) * dt
    expand = expand_ref[...]
    dt_e = _split3_dot(dt, expand)
    eacum_e = _split3_dot(eacum, expand)
    decay_e = _split3_dot(decay, expand)
    dtx = (dt_e * xs).astype(BF16)
    dxs = (decay_e * xs).astype(BF16)

    li = lax.broadcasted_iota(jnp.int32, (Q, Q), 0)
    si = lax.broadcasted_iota(jnp.int32, (Q, Q), 1)
    causal = li >= si
    half = lax.broadcasted_iota(jnp.int32, (Q, 2 * SSM_HEAD_DIM), 1) < SSM_HEAD_DIM

    for g in range(SSM_GROUPS):
        gs = slice(g * SSD_GW, (g + 1) * SSD_GW)
        bg = xc[:, D_INNER + g * D_STATE:D_INNER + (g + 1) * D_STATE]
        cg = xc[:, D_INNER + SSM_GROUPS * D_STATE + g * D_STATE:D_INNER + SSM_GROUPS * D_STATE + (g + 1) * D_STATE]
        bg16, cg16 = bg.astype(BF16), cg.astype(BF16)
        cbm = _nt_dot(cg16, bg16)
        ht = ht_ref[g]
        y = jnp.dot(cg16, ht.astype(BF16), preferred_element_type=F32) * eacum_e[:, gs]
        pieces = []
        for i in range(SSM_REP // 2):
            ws = []
            for h in (g * SSM_REP + 2 * i, g * SSM_REP + 2 * i + 1):
                seg = acum[:, h:h + 1] - acum_t[h:h + 1, :]
                ws.append((cbm * jnp.exp(jnp.where(causal, seg, NEG_INF))).astype(BF16))
            dpair = dtx[:, g * SSD_GW + i * 2 * SSM_HEAD_DIM:g * SSD_GW + (i + 1) * 2 * SSM_HEAD_DIM]
            pieces.append(jnp.where(half, jnp.dot(ws[0], dpair, preferred_element_type=F32),
                                    jnp.dot(ws[1], dpair, preferred_element_type=F32)))
        y = jnp.concatenate(pieces, axis=1) + y + dsk_ref[:, gs] * xs[:, gs]
        zg = z_ref[0, :, gs]
        y = y * (zg * jax.nn.sigmoid(zg))
        y = y * lax.rsqrt(jnp.mean(y * y, axis=-1, keepdims=True) + RMS_EPS)
        y_ref[0, :, gs] = (y * nw_ref[:, gs]).astype(y_ref.dtype)
        ht_ref[g] = eacum_e[Q - 1:Q, gs] * ht + jnp.dot(bg.T.astype(BF16), dxs[:, gs], preferred_element_type=F32)

    @pl.when(c == pl.num_programs(1) - 1)
    def _():
        for g in range(SSM_GROUPS):
            hfin_ref[0, g * SSD_GW:(g + 1) * SSD_GW, :] = ht_ref[g].T


def ssd_prompt(xbc, z, dt_raw, conv_w, conv_b, dt_bias, a_log, d_skip, norm_w):
    b, T, _ = xbc.shape
    assert T % SSD_CHUNK == 0 and dt_raw.shape[-1] == DT_PAD
    pad = lambda v: jnp.pad(v.astype(F32), (0, DT_PAD - SSM_HEADS)).reshape(1, DT_PAD)
    head_of = np.arange(D_INNER) // SSM_HEAD_DIM
    expand = jnp.asarray(np.arange(DT_PAD)[:, None] == head_of[None, :], BF16)
    tril = jnp.asarray(np.tril(np.ones((SSD_CHUNK, SSD_CHUNK))), BF16)
    dsk = jnp.repeat(d_skip.astype(F32), SSM_HEAD_DIM).reshape(1, D_INNER)
    const = lambda shape: pl.BlockSpec(shape, lambda bi, c: (0,) * len(shape))
    y, hfin = pl.pallas_call(
        _ssd_prompt_kernel,
        grid=(b, T // SSD_CHUNK),
        in_specs=[pl.BlockSpec((1, SSD_CHUNK, CONV_DIM), lambda bi, c: (bi, c, 0)),
                  pl.BlockSpec((1, SSD_CHUNK, D_INNER), lambda bi, c: (bi, c, 0)),
                  pl.BlockSpec((1, SSD_CHUNK, DT_PAD), lambda bi, c: (bi, c, 0)),
                  const((CONV_W, CONV_DIM)), const((1, CONV_DIM)), const((1, DT_PAD)), const((1, DT_PAD)),
                  const((1, D_INNER)), const((1, D_INNER)), const((DT_PAD, D_INNER)),
                  const((SSD_CHUNK, SSD_CHUNK))],
        out_specs=[pl.BlockSpec((1, SSD_CHUNK, D_INNER), lambda bi, c: (bi, c, 0)),
                   pl.BlockSpec((1, D_INNER, D_STATE), lambda bi, c: (bi, 0, 0))],
        out_shape=[jax.ShapeDtypeStruct((b, T, D_INNER), BF16),
                   jax.ShapeDtypeStruct((b, D_INNER, D_STATE), F32)],
        scratch_shapes=[pltpu.VMEM((SSD_CHUNK, CONV_DIM), F32),
                        pltpu.VMEM((SSM_GROUPS, D_STATE, SSD_GW), F32)],
        compiler_params=pltpu.CompilerParams(dimension_semantics=("parallel", "arbitrary"),
                                             vmem_limit_bytes=V7X_VMEM_LIMIT_BYTES),
        name="ssd_prompt",
    )(xbc, z, dt_raw, conv_w.astype(F32), conv_b.astype(F32).reshape(1, CONV_DIM), pad(dt_bias),
      pad(-jnp.exp(a_log.astype(F32))), dsk, norm_w.astype(F32).reshape(1, D_INNER), expand, tril)
    return y, hfin.reshape(b, SSM_HEADS, SSM_HEAD_DIM, D_STATE)


MOE_TM = 1024
MOE_TN = 512
MOE_TK = 1024


def _moe_up_kernel(te_ref, nu_ref, x_ref, wg_ref, wu_ref, h_ref):
    @pl.when(pl.program_id(0) < nu_ref[0])
    def _():
        x = x_ref[...]
        g = jnp.dot(x, wg_ref[0].astype(BF16), preferred_element_type=F32)
        u = jnp.dot(x, wu_ref[0].astype(BF16), preferred_element_type=F32)
        h_ref[...] = (g * jax.nn.sigmoid(g) * u).astype(h_ref.dtype)


def _moe_down_kernel(te_ref, nu_ref, h_ref, wd_ref, gw_ref, y_ref):
    k = pl.program_id(1)

    @pl.when(pl.program_id(0) < nu_ref[0])
    def _():
        part = jnp.dot(h_ref[...], wd_ref[0].astype(BF16), preferred_element_type=F32)

        @pl.when(k == 0)
        def _():
            y_ref[...] = part

        @pl.when(k > 0)
        def _():
            y_ref[...] += part

        @pl.when(k == pl.num_programs(1) - 1)
        def _():
            y_ref[...] = y_ref[...] * gw_ref[...]


def moe_grouped(x_sorted, row_w, tile_expert, n_used, wg, wu, wd, tm):
    P, D = x_sorted.shape
    E, _, FF = wg.shape
    tn = min(MOE_TN, FF)
    tk = min(MOE_TK, FF)
    assert P % tm == 0 and FF % tn == 0 and FF % tk == 0
    nt, nj, nk = P // tm, FF // tn, FF // tk

    def row(i, nu):
        return jnp.minimum(i, nu[0] - 1)

    def col(i, j, nu, last):
        return jnp.where(i < nu[0], j, last)

    h = pl.pallas_call(
        _moe_up_kernel,
        grid_spec=pltpu.PrefetchScalarGridSpec(
            num_scalar_prefetch=2, grid=(nt, nj),
            in_specs=[pl.BlockSpec((tm, D), lambda i, j, te, nu: (row(i, nu), 0)),
                      pl.BlockSpec((1, D, tn), lambda i, j, te, nu: (te[row(i, nu)], 0, col(i, j, nu, nj - 1))),
                      pl.BlockSpec((1, D, tn), lambda i, j, te, nu: (te[row(i, nu)], 0, col(i, j, nu, nj - 1)))],
            out_specs=pl.BlockSpec((tm, tn), lambda i, j, te, nu: (row(i, nu), col(i, j, nu, nj - 1)))),
        out_shape=jax.ShapeDtypeStruct((P, FF), BF16),
        compiler_params=pltpu.CompilerParams(dimension_semantics=("arbitrary", "arbitrary"),
                                             vmem_limit_bytes=V7X_VMEM_LIMIT_BYTES),
        name="moe_up",
    )(tile_expert, n_used, x_sorted, wg, wu)
    return pl.pallas_call(
        _moe_down_kernel,
        grid_spec=pltpu.PrefetchScalarGridSpec(
            num_scalar_prefetch=2, grid=(nt, nk),
            in_specs=[pl.BlockSpec((tm, tk), lambda i, k, te, nu: (row(i, nu), col(i, k, nu, nk - 1))),
                      pl.BlockSpec((1, tk, D), lambda i, k, te, nu: (te[row(i, nu)], col(i, k, nu, nk - 1), 0)),
                      pl.BlockSpec((tm, 1), lambda i, k, te, nu: (row(i, nu), 0))],
            out_specs=pl.BlockSpec((tm, D), lambda i, k, te, nu: (row(i, nu), 0))),
        out_shape=jax.ShapeDtypeStruct((P, D), F32),
        compiler_params=pltpu.CompilerParams(dimension_semantics=("arbitrary", "arbitrary"),
                                             vmem_limit_bytes=V7X_VMEM_LIMIT_BYTES),
        name="moe_down",
    )(tile_expert, n_used, h, wd, row_w)


def moe_topk(u, w_router, wg, wu, wd, tm=MOE_TM):
    N, D = u.shape
    E = wg.shape[0]
    logits = jnp.dot(u, w_router, precision=lax.Precision.HIGHEST)
    top_l, top_i = lax.top_k(logits, TOP_K)
    top_w = jax.nn.softmax(top_l, axis=-1)
    flat_e = top_i.reshape(-1)
    onehot = (flat_e[:, None] == jnp.arange(E, dtype=flat_e.dtype)[None, :]).astype(jnp.int32)
    csum = jnp.cumsum(onehot, axis=0)
    rank = jnp.take_along_axis(csum, flat_e[:, None], axis=1)[:, 0] - 1
    tiles_per_e = (csum[-1] + tm - 1) // tm
    tile_end = jnp.cumsum(tiles_per_e)
    dest = (tile_end - tiles_per_e)[flat_e] * tm + rank
    nt = (TOP_K * N + E * (tm - 1)) // tm + 1
    P = nt * tm
    tile_expert = jnp.minimum(jnp.searchsorted(tile_end, jnp.arange(nt), side='right'), E - 1).astype(jnp.int32)
    n_used = tile_end[-1:].astype(jnp.int32)
    row_token = jnp.zeros((P,), jnp.int32).at[dest].set(jnp.arange(TOP_K * N, dtype=jnp.int32) // TOP_K)
    row_w = jnp.zeros((P, 1), F32).at[dest, 0].set(top_w.reshape(-1))
    x_sorted = u.astype(BF16).at[row_token].get(mode='promise_in_bounds')
    y_sorted = moe_grouped(x_sorted, row_w, tile_expert, n_used, wg, wu, wd, tm)
    dest_k = dest.reshape(N, TOP_K)
    y = y_sorted.at[dest_k[:, 0]].get(mode='promise_in_bounds')
    for k in range(1, TOP_K):
        y = y + y_sorted.at[dest_k[:, k]].get(mode='promise_in_bounds')
    return y


def rmsnorm(x, w):
    xf = x.astype(F32)
    y = xf * lax.rsqrt(jnp.mean(xf * xf, axis=-1, keepdims=True) + RMS_EPS)
    return (y * w.astype(F32)).astype(x.dtype)


def masked_softmax(s, mask):
    p = jax.nn.softmax(jnp.where(mask, s, NEG_INF), axis=-1)
    return p * mask


def rotary(x, pos):
    half = ROT_DIM // 2
    inv = ROPE_THETA ** (-jnp.arange(half, dtype=F32) / half)
    ang = pos.astype(F32)[:, None] * inv[None, :]
    ang = ang.reshape((1, pos.shape[0]) + (1,) * (x.ndim - 3) + (half,))
    cos, sin = jnp.cos(ang), jnp.sin(ang)
    xr = x[..., :ROT_DIM].astype(F32)
    x1, x2 = xr[..., :half], xr[..., half:]
    rot = jnp.concatenate([x1 * cos - x2 * sin, x2 * cos + x1 * sin], axis=-1)
    return jnp.concatenate([rot.astype(x.dtype), x[..., ROT_DIM:]], axis=-1)


def split_in(proj):
    cuts = np.cumsum(IN_SIZES)[:-1].tolist()
    return jnp.split(proj, cuts, axis=-1)


def ssd_chunked(x, dt, a, bm, cm, h0):
    b, T = x.shape[:2]
    q = SSD_CHUNK if T >= SSD_CHUNK else T
    nc = -(-T // q)
    pad = nc * q - T

    def chunks(v):
        v = jnp.pad(v.astype(F32), [(0, 0), (0, pad)] + [(0, 0)] * (v.ndim - 2))
        return jnp.moveaxis(v.reshape((b, nc, q) + v.shape[2:]), 1, 0)

    causal = jnp.tril(jnp.ones((q, q), bool))[None, :, :, None, None]

    def step(h, inp):
        xc, dtc, bc, cc = inp
        acum = jnp.cumsum(dtc * a, axis=1)
        seg = acum[:, :, None] - acum[:, None, :]
        lmat = jnp.exp(jnp.where(causal, seg, -jnp.inf))
        cb = jnp.einsum('blgn,bsgn->blsg', cc, bc)
        y_diag = jnp.einsum('blsg,blsgr,bsgrp->blgrp', cb, lmat, dtc[..., None] * xc)
        y_off = jnp.einsum('blgn,bgrpn->blgrp', cc, h) * jnp.exp(acum)[..., None]
        decay = jnp.exp(acum[:, -1:] - acum) * dtc
        h_new = jnp.exp(acum[:, -1])[..., None, None] * h + jnp.einsum('bsgn,bsgr,bsgrp->bgrpn', bc, decay, xc)
        return h_new, y_diag + y_off

    h_fin, ys = lax.scan(step, h0.astype(F32), (chunks(x), chunks(dt), chunks(bm), chunks(cm)))
    y = jnp.moveaxis(ys, 0, 1).reshape((b, nc * q) + x.shape[2:])[:, :T]
    return y, h_fin


def ssm_branch(z, xbc, dt_raw, conv0, h0, conv_w, conv_b, dt_bias, a_log, d_skip, norm_w):
    b, T, _ = xbc.shape
    xpad = jnp.concatenate([conv0.astype(xbc.dtype), xbc], axis=1)
    acc = conv_b
    for k in range(CONV_W):
        acc = acc + xpad[:, k:k + T] * conv_w[k]
    xbc_c = jax.nn.silu(acc)
    new_conv = xpad[:, T:]
    xs, bm, cm = jnp.split(xbc_c, [D_INNER, D_INNER + SSM_GROUPS * D_STATE], axis=-1)
    xs = xs.reshape(b, T, SSM_GROUPS, SSM_REP, SSM_HEAD_DIM)
    bm = bm.reshape(b, T, SSM_GROUPS, D_STATE)
    cm = cm.reshape(b, T, SSM_GROUPS, D_STATE)
    dt = jax.nn.softplus(dt_raw.astype(F32) + dt_bias.astype(F32)).reshape(b, T, SSM_GROUPS, SSM_REP)
    a = -jnp.exp(a_log.astype(F32)).reshape(SSM_GROUPS, SSM_REP)
    h0 = h0.reshape(b, SSM_GROUPS, SSM_REP, SSM_HEAD_DIM, D_STATE)
    y, h_fin = ssd_chunked(xs, dt, a, bm, cm, h0)
    y = y + d_skip.astype(F32).reshape(SSM_GROUPS, SSM_REP)[..., None] * xs.astype(F32)
    y = y.reshape(b, T, D_INNER) * jax.nn.silu(z.astype(F32))
    yg = y.reshape(b, T, SSM_GROUPS, D_INNER // SSM_GROUPS)
    yg = yg * lax.rsqrt(jnp.mean(yg * yg, axis=-1, keepdims=True) + RMS_EPS)
    y = yg.reshape(b, T, D_INNER) * norm_w.astype(F32)
    return y.astype(z.dtype), new_conv, h_fin.reshape(b, SSM_HEADS, SSM_HEAD_DIM, D_STATE)


def compress(rows, w1, w2, pe):
    b, L = rows.shape[:2]
    nf = L // CMP_STRIDE
    ch = rows[:, :nf * CMP_STRIDE].reshape(b, nf, CMP_STRIDE, NSA_KV_GROUPS, HEAD_DIM)
    first = jnp.einsum('bnsgd,sde->bnge', ch, w1[:CMP_STRIDE])
    second = jnp.einsum('bnsgd,sde->bnge', ch, w1[CMP_STRIDE:])
    pe_bias = jnp.einsum('ld,lde->e', pe, w1)
    hid = jax.nn.gelu(first[:, :-1] + second[:, 1:] + pe_bias)
    return jnp.einsum('bnge,ed->bngd', hid, w2)


def cmp_to_sel(n_cb, n_sb):
    i = jnp.arange(n_cb)[:, None] * CMP_STRIDE
    j = jnp.arange(n_sb)[None, :] * SEL_BLOCK
    return ((i < j + SEL_BLOCK) & (i + CMP_LEN > j)).astype(F32)


def cmp_attend(q, kc, vc, q_pos):
    n = kc.shape[1]
    s = jnp.einsum('btgrd,bngd->bgrtn', q, kc, preferred_element_type=F32) * ATTN_SCALE
    end = jnp.arange(n) * CMP_STRIDE + CMP_LEN - 1
    p = masked_softmax(s, end[None, :] <= q_pos[:, None])
    o = jnp.einsum('bgrtn,bngd->btgrd', p.astype(vc.dtype), vc)
    return o, p


def select_blocks(imp, q_pos):
    n_sb = imp.shape[-1]
    j = jnp.arange(n_sb)[None, :]
    cur = (q_pos // SEL_BLOCK)[:, None]
    valid = (j * SEL_BLOCK <= q_pos[:, None])[:, None, :]
    forced = ((j == 0) | (j == cur) | (j == cur - 1))[:, None, :]
    score = jnp.where(valid, imp + jnp.where(forced, FORCE_BONUS, 0.0), NEG_INF)
    top, idx = lax.top_k(score, min(N_SEL, n_sb))
    return idx, top > 0.5 * NEG_INF


def slc_attend(q, kb, vb, idx, sel_ok, q_pos):
    kpos = idx[..., None] * SEL_BLOCK + jnp.arange(SEL_BLOCK)
    mask = sel_ok[..., None] & (kpos <= q_pos[None, :, None, None, None])
    s = jnp.einsum('btgrd,btgkld->btgrkl', q, kb, preferred_element_type=F32) * ATTN_SCALE
    b, T, G, R, K, L = s.shape
    p = masked_softmax(s.reshape(b, T, G, R, K * L), mask.reshape(b, T, G, 1, K * L))
    return jnp.einsum('btgrkl,btgkld->btgrd', p.reshape(s.shape).astype(vb.dtype), vb)


def slc_prompt(q, k, v, idx, sel_ok, q_pos):
    b, T = q.shape[:2]
    nsb = T // SEL_BLOCK
    kblk = k.reshape(b, nsb, SEL_BLOCK, NSA_KV_GROUPS, HEAD_DIM)
    vblk = v.reshape(b, nsb, SEL_BLOCK, NSA_KV_GROUPS, HEAD_DIM)
    bi = jnp.arange(b)[:, None, None, None]
    gi = jnp.arange(NSA_KV_GROUPS)[None, None, :, None]
    nq = T // SLC_QCHUNK

    def chunked(a):
        return jnp.moveaxis(a.reshape((b, nq, SLC_QCHUNK) + a.shape[2:]), 1, 0)

    def body(args):
        qc, ic, okc, pc = args
        kb = kblk[bi, ic, :, gi, :]
        vb = vblk[bi, ic, :, gi, :]
        return slc_attend(qc, kb, vb, ic, okc, pc)

    out = lax.map(body, (chunked(q), chunked(idx), chunked(sel_ok), q_pos.reshape(nq, SLC_QCHUNK)))
    return jnp.moveaxis(out, 0, 1).reshape(q.shape)


def _block_gather_kernel(blk_ref, *refs):
    G = NSA_KV_GROUPS
    k_refs, v_refs, ko_ref, vo_ref = refs[:G], refs[G:2 * G], refs[2 * G], refs[2 * G + 1]
    for g in range(G):
        ko_ref[0, g, 0] = k_refs[g][0, 0, pl.ds(g, SEL_BLOCK, stride=G), :]
        vo_ref[0, g, 0] = v_refs[g][0, 0, pl.ds(g, SEL_BLOCK, stride=G), :]


def gather_pool_blocks(pool_k, pool_v, layer, blk):
    b, G, K = blk.shape
    view = lambda p: p.reshape(p.shape[0], p.shape[1] * (PAGE_SIZE // SEL_BLOCK), SEL_BLOCK * G, HEAD_DIM)

    def spec(g):
        return pl.BlockSpec((1, 1, SEL_BLOCK * G, HEAD_DIM),
                            lambda bi, k, ids: (layer, ids[(bi * G + g) * K + k], 0, 0))

    out_spec = pl.BlockSpec((1, G, 1, SEL_BLOCK, HEAD_DIM), lambda bi, k, ids: (bi, 0, k, 0, 0))
    out_shape = jax.ShapeDtypeStruct((b, G, K, SEL_BLOCK, HEAD_DIM), pool_k.dtype)
    return pl.pallas_call(
        _block_gather_kernel,
        grid_spec=pltpu.PrefetchScalarGridSpec(
            num_scalar_prefetch=1, grid=(b, K),
            in_specs=[spec(g) for g in range(G)] * 2,
            out_specs=[out_spec, out_spec]),
        out_shape=[out_shape, out_shape],
        compiler_params=pltpu.CompilerParams(dimension_semantics=("parallel", "arbitrary"),
                                             vmem_limit_bytes=V7X_VMEM_LIMIT_BYTES),
        name="block_gather",
    )(blk.reshape(-1), *([view(pool_k)] * G), *([view(pool_v)] * G))


def gather_selected(from_past, new_rows, idx, past):
    b, S = new_rows.shape[:2]
    n_past_blk = past // SEL_BLOCK
    n_new_blk = -(-S // SEL_BLOCK)
    bi = jnp.arange(b)[:, None, None, None]
    gi = jnp.arange(NSA_KV_GROUPS)[None, None, :, None]
    new_blk = jnp.pad(new_rows, ((0, 0), (0, n_new_blk * SEL_BLOCK - S), (0, 0), (0, 0)))
    new_blk = new_blk.reshape(b, n_new_blk, SEL_BLOCK, NSA_KV_GROUPS, HEAD_DIM)
    jn = jnp.clip(idx - n_past_blk, 0, n_new_blk - 1)
    from_new = new_blk[bi, jn, :, gi, :]
    return jnp.where((idx < n_past_blk)[..., None, None], from_past, from_new.astype(from_past.dtype))


def window_attend(q, k, v, q_pos, k_pos):
    s = jnp.einsum('btgrd,bsgd->bgrts', q, k, preferred_element_type=F32) * ATTN_SCALE
    diff = q_pos[:, None] - k_pos[None, :]
    mask = (diff >= 0) & (diff < WINDOW) & (k_pos[None, :] >= 0)
    p = masked_softmax(s, mask)
    return jnp.einsum('bgrts,bsgd->btgrd', p.astype(v.dtype), v)


def win_prompt(q, k, v):
    b, T = q.shape[:2]
    nq = T // WIN_QBLOCK
    span = WIN_QBLOCK + WINDOW
    kp = jnp.pad(k, ((0, 0), (WINDOW, 0), (0, 0), (0, 0)))
    vp = jnp.pad(v, ((0, 0), (WINDOW, 0), (0, 0), (0, 0)))
    qb = jnp.moveaxis(q.reshape((b, nq, WIN_QBLOCK) + q.shape[2:]), 1, 0)

    def body(args):
        i, qc = args
        start = i * WIN_QBLOCK
        kc = lax.dynamic_slice_in_dim(kp, start, span, axis=1)
        vc = lax.dynamic_slice_in_dim(vp, start, span, axis=1)
        q_pos = start + jnp.arange(WIN_QBLOCK)
        k_pos = start - WINDOW + jnp.arange(span)
        return window_attend(qc, kc, vc, q_pos, k_pos)

    out = lax.map(body, (jnp.arange(nq), qb))
    return jnp.moveaxis(out, 0, 1).reshape(q.shape)


def combine_nsa(gate, o_cmp, o_slc, o_win, dtype):
    b, T = o_cmp.shape[:2]
    o = gate[:, :, 0] * o_cmp + gate[:, :, 1] * o_slc + gate[:, :, 2] * o_win
    return o.reshape(b, T, NSA_HEADS * HEAD_DIM).astype(dtype)


def nsa_prompt(q, kvs, g_nsa, phi):
    phi1_k, phi2_k, pe_k, phi1_v, phi2_v, pe_v = phi
    b, T, _ = q.shape
    cos_t, sin_t = rope_tables(jnp.arange(T, dtype=jnp.int32))
    heads = lambda x: x.reshape(b, T, NSA_KV_GROUPS, HEAD_DIM)
    rot = lambda x, dt: rope_rows(x.reshape(b * T, x.shape[-1]), cos_t, sin_t, dt).reshape(x.shape)
    q = rot(q, BF16)
    k_cmp, k_slc, k_win = heads(rot(kvs[0], F32)), heads(rot(kvs[2], F32)), heads(rot(kvs[4], F32))
    v_cmp, v_slc, v_win = heads(kvs[1]), heads(kvs[3]), heads(kvs[5])
    assert T % (PAGE_SIZE * CMP_PAGES) == 0
    pages = jnp.arange(b * T // PAGE_SIZE, dtype=jnp.int32)
    as_pool = lambda x: x.reshape(1, b * T // PAGE_SIZE, PAGE_SIZE * NSA_KV_GROUPS, HEAD_DIM)
    kc = compress_finish(compress_first(as_pool(k_cmp), 0, pages, b, phi1_k), phi1_k, phi2_k, pe_k)
    vc = compress_finish(compress_first(as_pool(v_cmp), 0, pages, b, phi1_v), phi1_v, phi2_v, pe_v)
    o = nsa_prompt_attention(q, kc, vc, k_slc, v_slc, k_win, v_win, g_nsa)
    wb = min(WINDOW, T)
    return o, (k_cmp, v_cmp, k_slc, v_slc, k_win[:, T - wb:], v_win[:, T - wb:])


def nsa_sample(q, kv, g_nsa, phi, layer, pool_k_cmp, pool_v_cmp, ck_slc, cv_slc, ck_win, cv_win, page_table):
    phi1_k, phi2_k, pe_k, phi1_v, phi2_v, pe_v = phi
    b, S = q.shape[:2]
    assert S < CMP_STRIDE
    gate = jax.nn.sigmoid(g_nsa.astype(F32)).reshape(b, S, 3, NSA_KV_GROUPS, NSA_REP, 1)
    past = page_table.shape[1] * PAGE_SIZE
    pos = past + jnp.arange(S, dtype=jnp.int32)
    q = rotary(q, pos)
    k_cmp, k_slc, k_win = rotary(kv[:, :, 0], pos), rotary(kv[:, :, 2], pos), rotary(kv[:, :, 4], pos)
    v_cmp, v_slc, v_win = kv[:, :, 1], kv[:, :, 3], kv[:, :, 5]

    def compress_past(pool, w1, w2, pe):
        pool = pool.reshape(pool.shape[:2] + (PAGE_SIZE * NSA_KV_GROUPS, HEAD_DIM))
        blocks = compress_finish(compress_first(pool, layer, page_table.reshape(-1), b, w1), w1, w2, pe)
        return jnp.transpose(blocks, (0, 2, 1, 3))

    kc = compress_past(pool_k_cmp, phi1_k, phi2_k, pe_k)
    vc = compress_past(pool_v_cmp, phi1_v, phi2_v, pe_v)
    wb = ck_win.shape[1]
    k_all = jnp.concatenate([ck_win.astype(k_win.dtype), k_win], axis=1)
    v_all = jnp.concatenate([cv_win.astype(v_win.dtype), v_win], axis=1)
    k_pos = past - wb + jnp.arange(wb + S)
    with jax.default_matmul_precision("highest"):
        o_cmp, p_cmp = cmp_attend(q, kc, vc, pos)
        imp = jnp.einsum('bgrtn,nj->btgj', p_cmp, cmp_to_sel(kc.shape[1], -(-(past + S) // SEL_BLOCK)))
        idx, sel_ok = select_blocks(imp, pos)
        bpp = PAGE_SIZE // SEL_BLOCK
        jp = jnp.minimum(idx, past // SEL_BLOCK - 1)[:, 0]
        blk = jnp.take_along_axis(page_table, (jp // bpp).reshape(b, -1), axis=1).reshape(jp.shape) * bpp + jp % bpp
        kb, vb = gather_pool_blocks(ck_slc, cv_slc, layer, blk.astype(jnp.int32))
        kb = gather_selected(kb[:, None], k_slc, idx, past)
        vb = gather_selected(vb[:, None], v_slc, idx, past)
        o_slc = slc_attend(q, kb, vb, idx, sel_ok, pos)
        o_win = window_attend(q, k_all, v_all, pos, k_pos)
    o = combine_nsa(gate, o_cmp, o_slc, o_win, q.dtype)
    return o, (k_cmp, v_cmp, k_slc, v_slc, k_all[:, -wb:], v_all[:, -wb:])


def token_mixer(h, conv0, ssm0, nsa_fn, norm_w, w_in, conv_w, conv_b, dt_bias, a_log, d_skip,
                ssm_norm_w, w_br_ssm, w_br_nsa, w_out):
    b, T, _ = h.shape
    sample = conv0 is not None
    u = rmsnorm(h, norm_w)
    if not sample:
        u = u.astype(BF16)
    proj = functools.partial(mm, precise=sample)
    o = np.cumsum((0,) + IN_SIZES)
    z, xbc, q, g_nsa, g_merge = (proj(u, w_in[:, o[i]:o[i + 1]]) for i in (0, 1, 3, 5, 6))
    dt_pad = proj(u, w_in[:, o[2]:o[2] + DT_PAD])
    if not sample:
        y_ssm, ssm_new = ssd_prompt(xbc, z, dt_pad, conv_w, conv_b, dt_bias, a_log, d_skip, ssm_norm_w)
        conv_new = xbc[:, T - (CONV_W - 1):]
        kvs = [proj(u, w_in[:, o[4] + i * KV_WIDTH:o[4] + (i + 1) * KV_WIDTH]) for i in range(6)]
        o_nsa, nsa_state = nsa_fn(q, kvs, g_nsa)
    else:
        with jax.default_matmul_precision("highest"):
            y_ssm, conv_new, ssm_new = ssm_branch(z, xbc, dt_pad[..., :SSM_HEADS], conv0, ssm0, conv_w, conv_b,
                                                  dt_bias, a_log, d_skip, ssm_norm_w)
        q = q.reshape(b, T, NSA_KV_GROUPS, NSA_REP, HEAD_DIM)
        kv = proj(u, w_in[:, o[4]:o[5]]).reshape(b, T, 6, NSA_KV_GROUPS, HEAD_DIM)
        o_nsa, nsa_state = nsa_fn(q, kv, g_nsa)
    rows = lambda x: x.reshape(b * T, x.shape[-1])
    merged = gated_merge(rows(y_ssm), w_br_ssm, rows(o_nsa), w_br_nsa, rows(g_merge), precise=sample)
    out = pmm(merged, w_out, precise=sample).reshape(b, T, D_MODEL)
    return out, nsa_state + (ssm_new.astype(h.dtype), conv_new)


def swiglu(u, wg, wu, wd, precise=False):
    lead = u.shape[:-1]
    hid = swiglu_up(u.reshape(-1, u.shape[-1]), wg, wu, precise=precise)
    return pmm(hid, wd, precise=precise).reshape(lead + (wd.shape[-1],))


def channel_mixer(hp, hs, l, norm_w, w_gate, w_up, w_down, w_router, w_gate_e, w_up_e, w_down_e):
    up, us = rmsnorm(hp, norm_w), rmsnorm(hs, norm_w)
    i = l // 2
    if l % 2 == 0:
        return (swiglu(up.astype(BF16), w_gate[i], w_up[i], w_down[i]),
                swiglu(us, w_gate[i], w_up[i], w_down[i], precise=True))
    n_p = up.shape[0] * up.shape[1]
    u_all = jnp.concatenate([up.reshape(n_p, D_MODEL), us.reshape(-1, D_MODEL)], axis=0)
    y = moe_topk(u_all, w_router[i], w_gate_e[i], w_up_e[i], w_down_e[i])
    return y[:n_p].reshape(up.shape), y[n_p:].reshape(us.shape)


def kernel(x_prompt, x_sample, cache_k_cmp, cache_v_cmp, cache_k_slc, cache_v_slc, cache_k_win, cache_v_win, state_ssm, state_conv, page_table, norm_mix, w_in, conv_w, conv_b, dt_bias, a_log, d_skip, ssm_norm, phi1_k, phi2_k, pe_k, phi1_v, phi2_v, pe_v, w_br_ssm, w_br_nsa, w_out, norm_ffn, w_gate, w_up, w_down, w_router, w_gate_e, w_up_e, w_down_e, norm_final):
    hp, hs = x_prompt, x_sample
    bp = x_prompt.shape[0]
    p_states = [[] for _ in range(8)]
    s_states = [[] for _ in range(8)]
    for l in range(DEPTH):
        mw = (norm_mix[l], w_in[l], conv_w[l], conv_b[l], dt_bias[l], a_log[l], d_skip[l],
              ssm_norm[l], w_br_ssm[l], w_br_nsa[l], w_out[l])
        phi = (phi1_k[l], phi2_k[l], pe_k[l], phi1_v[l], phi2_v[l], pe_v[l])
        ffn = (w_gate, w_up, w_down, w_router, w_gate_e, w_up_e, w_down_e)
        out, st = token_mixer(hp, None, None, functools.partial(nsa_prompt, phi=phi), *mw)
        hp = hp + out
        for i in range(8):
            p_states[i].append(st[i])
        nsa_fn = functools.partial(nsa_sample, phi=phi, layer=l, pool_k_cmp=cache_k_cmp, pool_v_cmp=cache_v_cmp,
                                   ck_slc=cache_k_slc, cv_slc=cache_v_slc,
                                   ck_win=cache_k_win[l], cv_win=cache_v_win[l], page_table=page_table)
        out, st = token_mixer(hs, state_conv[l], state_ssm[l], nsa_fn, *mw)
        hs = hs + out
        for i in range(8):
            s_states[i].append(st[i])
        fp, fs = channel_mixer(hp, hs, l, norm_ffn[l], *ffn)
        hp, hs = hp + fp, hs + fs
    y_prompt = rmsnorm(hp, norm_final)
    y_sample = rmsnorm(hs, norm_final)
    p_out = [jnp.stack(a) for a in p_states]
    s_out = [jnp.stack(a) for a in s_states]
    return (y_prompt, y_sample, *p_out, *s_out)
```

```python
import functools
import math

import jax
import jax.numpy as jnp
import numpy as np
from jax import lax
from jax.experimental import pallas as pl
from jax.experimental.pallas import tpu as pltpu

F32 = jnp.float32
BF16 = jnp.bfloat16

D_MODEL = 2048
DEPTH = 2
PAGE_SIZE = 128
D_INNER = 4096
SSM_HEAD_DIM = 64
SSM_HEADS = 64
SSM_GROUPS = 8
SSM_REP = 8
D_STATE = 128
CONV_W = 4
CONV_DIM = D_INNER + 2 * SSM_GROUPS * D_STATE
SSD_CHUNK = 128
NSA_HEADS = 16
NSA_KV_GROUPS = 4
NSA_REP = 4
HEAD_DIM = 128
KV_WIDTH = NSA_KV_GROUPS * HEAD_DIM
ROT_DIM = HEAD_DIM // 4
ROPE_THETA = 500000.0
ATTN_SCALE = HEAD_DIM ** -0.5
CMP_STRIDE = 16
CMP_LEN = 32
SEL_BLOCK = 64
SEL_SHIFT = 6
N_SEL = 16
WINDOW = 512
WIN_QBLOCK = 128
SLC_QCHUNK = 16
FORCE_BONUS = 1.0e4
NEG_INF = -1.0e30
N_EXPERTS = 8
TOP_K = 2
RMS_EPS = 1e-6
IN_SIZES = (D_INNER, CONV_DIM, SSM_HEADS, NSA_HEADS * HEAD_DIM, 6 * KV_WIDTH, 3 * NSA_HEADS, 2 * D_MODEL)

V7X_VMEM_LIMIT_BYTES = 56 * 1024 * 1024
LANE = 128


def _dot(x, w, precise):
    if precise:
        return jnp.dot(x.astype(F32), w.astype(F32), preferred_element_type=F32,
                       precision=lax.Precision.HIGHEST)
    return jnp.dot(x.astype(BF16), w.astype(BF16), preferred_element_type=F32)


def _mm_kernel(x_ref, w_ref, o_ref, acc_ref, *, precise):
    k = pl.program_id(2)

    @pl.when(k == 0)
    def _():
        acc_ref[...] = jnp.zeros_like(acc_ref)

    acc_ref[...] += _dot(x_ref[...], w_ref[...], precise)

    @pl.when(k == pl.num_programs(2) - 1)
    def _():
        o_ref[...] = acc_ref[...]


def _pick_tk(K, cap=2048):
    best = None
    for t in range(LANE, min(K, cap) + 1, LANE):
        if K % t == 0:
            best = t
    return best if best is not None else K


def pmm(x, w, tm=1024, tn=512, precise=False):
    M, K = x.shape
    K2, N = w.shape
    assert K == K2
    tm = min(tm, M)
    tn = min(tn, N)
    tk = _pick_tk(K)
    assert M % tm == 0
    grid = (M // tm, pl.cdiv(N, tn), K // tk)
    return pl.pallas_call(
        functools.partial(_mm_kernel, precise=precise),
        grid=grid,
        in_specs=[pl.BlockSpec((tm, tk), lambda i, j, k: (i, k)),
                  pl.BlockSpec((tk, tn), lambda i, j, k: (k, j))],
        out_specs=pl.BlockSpec((tm, tn), lambda i, j, k: (i, j)),
        out_shape=jax.ShapeDtypeStruct((M, N), F32),
        scratch_shapes=[pltpu.VMEM((tm, tn), F32)],
        compiler_params=pltpu.CompilerParams(
            dimension_semantics=("parallel", "parallel", "arbitrary"),
            vmem_limit_bytes=V7X_VMEM_LIMIT_BYTES),
        name="pmm",
    )(x, w)


def mm(x, w, precise=False):
    lead = x.shape[:-1]
    return pmm(x.reshape(-1, x.shape[-1]), w, precise=precise).reshape(lead + (w.shape[-1],))


def _merge_kernel(a_ref, wa_ref, b_ref, wb_ref, ga_ref, gb_ref, o_ref, *, precise):
    ya = _dot(a_ref[...], wa_ref[...], precise)
    yb = _dot(b_ref[...], wb_ref[...], precise)
    o_ref[...] = (jax.nn.sigmoid(ga_ref[...]) * ya + jax.nn.sigmoid(gb_ref[...]) * yb).astype(o_ref.dtype)


def gated_merge(a, wa, b, wb, g_merge, tm=1024, tn=256, precise=False):
    M, Ka = a.shape
    Kb = b.shape[1]
    N = wa.shape[1]
    tm, tn = min(tm, M), min(tn, N)
    assert M % tm == 0 and N % tn == 0 and g_merge.shape == (M, 2 * N)
    nj = N // tn
    return pl.pallas_call(
        functools.partial(_merge_kernel, precise=precise),
        grid=(M // tm, nj),
        in_specs=[pl.BlockSpec((tm, Ka), lambda i, j: (i, 0)),
                  pl.BlockSpec((Ka, tn), lambda i, j: (0, j)),
                  pl.BlockSpec((tm, Kb), lambda i, j: (i, 0)),
                  pl.BlockSpec((Kb, tn), lambda i, j: (0, j)),
                  pl.BlockSpec((tm, tn), lambda i, j: (i, j)),
                  pl.BlockSpec((tm, tn), lambda i, j: (i, j + nj))],
        out_specs=pl.BlockSpec((tm, tn), lambda i, j: (i, j)),
        out_shape=jax.ShapeDtypeStruct((M, N), F32 if precise else BF16),
        compiler_params=pltpu.CompilerParams(dimension_semantics=("parallel", "parallel"),
                                             vmem_limit_bytes=V7X_VMEM_LIMIT_BYTES),
        name="gated_merge",
    )(a, wa, b, wb, g_merge, g_merge)


def _swiglu_up_kernel(x_ref, wg_ref, wu_ref, h_ref, *, precise):
    x = x_ref[...]
    g = _dot(x, wg_ref[...], precise)
    u = _dot(x, wu_ref[...], precise)
    h_ref[...] = (g * jax.nn.sigmoid(g) * u).astype(h_ref.dtype)


def swiglu_up(x, wg, wu, tm=1024, tn=512, precise=False):
    M, D = x.shape
    FF = wg.shape[1]
    tm, tn = min(tm, M), min(tn, FF)
    assert M % tm == 0 and FF % tn == 0
    return pl.pallas_call(
        functools.partial(_swiglu_up_kernel, precise=precise),
        grid=(M // tm, FF // tn),
        in_specs=[pl.BlockSpec((tm, D), lambda i, j: (i, 0)),
                  pl.BlockSpec((D, tn), lambda i, j: (0, j)),
                  pl.BlockSpec((D, tn), lambda i, j: (0, j))],
        out_specs=pl.BlockSpec((tm, tn), lambda i, j: (i, j)),
        out_shape=jax.ShapeDtypeStruct((M, FF), F32 if precise else BF16),
        compiler_params=pltpu.CompilerParams(dimension_semantics=("parallel", "parallel"),
                                             vmem_limit_bytes=V7X_VMEM_LIMIT_BYTES),
        name="swiglu_up",
    )(x, wg, wu)


ROPE_TR = 512


def _rope_kernel(x_ref, cos_ref, sin_ref, o_ref):
    cos, sin = cos_ref[...], sin_ref[...]
    lane = lax.broadcasted_iota(jnp.int32, cos.shape, 1)
    first = lane < ROT_DIM // 2
    for h in range(x_ref.shape[1] // HEAD_DIM):
        x = x_ref[:, h * HEAD_DIM:(h + 1) * HEAD_DIM].astype(F32)
        partner = jnp.where(first, pltpu.roll(x, HEAD_DIM - ROT_DIM // 2, 1), pltpu.roll(x, ROT_DIM // 2, 1))
        o_ref[:, h * HEAD_DIM:(h + 1) * HEAD_DIM] = (x * cos + partner * sin).astype(o_ref.dtype)


def rope_tables(pos):
    half = ROT_DIM // 2
    inv = ROPE_THETA ** (-jnp.arange(half, dtype=F32) / half)
    ang = pos.astype(F32)[:, None] * inv[None, :]
    cos, sin = jnp.cos(ang), jnp.sin(ang)
    n = pos.shape[0]
    cos_t = jnp.concatenate([cos, cos, jnp.ones((n, HEAD_DIM - ROT_DIM), F32)], axis=1)
    sin_t = jnp.concatenate([-sin, sin, jnp.zeros((n, HEAD_DIM - ROT_DIM), F32)], axis=1)
    return cos_t, sin_t


def rope_rows(x, cos_t, sin_t, out_dtype):
    M, C = x.shape
    T = cos_t.shape[0]
    tr = min(ROPE_TR, T)
    assert T % tr == 0 and M % T == 0 and C % HEAD_DIM == 0
    nt = T // tr
    return pl.pallas_call(
        _rope_kernel,
        grid=(M // tr,),
        in_specs=[pl.BlockSpec((tr, C), lambda i: (i, 0)),
                  pl.BlockSpec((tr, HEAD_DIM), lambda i: (i % nt, 0)),
                  pl.BlockSpec((tr, HEAD_DIM), lambda i: (i % nt, 0))],
        out_specs=pl.BlockSpec((tr, C), lambda i: (i, 0)),
        out_shape=jax.ShapeDtypeStruct((M, C), out_dtype),
        compiler_params=pltpu.CompilerParams(dimension_semantics=("parallel",),
                                             vmem_limit_bytes=V7X_VMEM_LIMIT_BYTES),
        name="rope",
    )(x, cos_t, sin_t)


CMP_PAGES = 8
CHUNKS_PER_PAGE = PAGE_SIZE // CMP_STRIDE
CHUNK_ROWS = CMP_STRIDE * NSA_KV_GROUPS
CMP_PITCH = CHUNK_ROWS + 4


def _compress_kernel(pt_ref, *refs):
    page_refs, w_ref, o_ref, stage_ref = refs[:CMP_PAGES], refs[CMP_PAGES], refs[CMP_PAGES + 1], refs[CMP_PAGES + 2]
    for p, p_ref in enumerate(page_refs):
        for n in range(CHUNKS_PER_PAGE):
            stage_ref[p, pl.ds(n * CMP_PITCH, CHUNK_ROWS), :] = p_ref[0, 0, n * CHUNK_ROWS:(n + 1) * CHUNK_ROWS, :]
    slabs = []
    for g in range(NSA_KV_GROUPS):
        for p in range(CMP_PAGES):
            slabs.append(jnp.concatenate(
                [stage_ref[p, pl.ds(s * NSA_KV_GROUPS + g, CHUNKS_PER_PAGE, stride=CMP_PITCH), :]
                 for s in range(CMP_STRIDE)], axis=1))
    x = jnp.concatenate(slabs, axis=0).astype(BF16)
    y = jnp.dot(x, w_ref[...], preferred_element_type=F32)
    per_g = CMP_PAGES * CHUNKS_PER_PAGE
    for g in range(NSA_KV_GROUPS):
        o_ref[0, g] = y[g * per_g:(g + 1) * per_g]


def compress_first(pool, layer, page_flat, n_seq, w1):
    ppb = page_flat.shape[0] // n_seq
    assert ppb % CMP_PAGES == 0
    phi_h = w1.shape[-1]
    w = jnp.concatenate([w1[:CMP_STRIDE].reshape(CMP_STRIDE * HEAD_DIM, phi_h),
                         w1[CMP_STRIDE:].reshape(CMP_STRIDE * HEAD_DIM, phi_h)], axis=1).astype(BF16)
    steps = ppb // CMP_PAGES
    per_g = CMP_PAGES * CHUNKS_PER_PAGE

    def page_spec(i):
        return pl.BlockSpec((1, 1, PAGE_SIZE * NSA_KV_GROUPS, HEAD_DIM),
                            lambda b, j, pt: (layer, pt[b * ppb + j * CMP_PAGES + i], 0, 0))

    return pl.pallas_call(
        _compress_kernel,
        grid_spec=pltpu.PrefetchScalarGridSpec(
            num_scalar_prefetch=1, grid=(n_seq, steps),
            in_specs=[page_spec(i) for i in range(CMP_PAGES)]
                     + [pl.BlockSpec(w.shape, lambda b, j, pt: (0, 0))],
            out_specs=pl.BlockSpec((1, NSA_KV_GROUPS, per_g, 2 * phi_h), lambda b, j, pt: (b, 0, j, 0)),
            scratch_shapes=[pltpu.VMEM((CMP_PAGES, CHUNKS_PER_PAGE * CMP_PITCH, HEAD_DIM), F32)]),
        out_shape=jax.ShapeDtypeStruct((n_seq, NSA_KV_GROUPS, ppb * CHUNKS_PER_PAGE, 2 * phi_h), F32),
        compiler_params=pltpu.CompilerParams(dimension_semantics=("parallel", "arbitrary"),
                                             vmem_limit_bytes=V7X_VMEM_LIMIT_BYTES),
        name="compress_first",
    )(page_flat, *([pool] * CMP_PAGES), w)


def compress_finish(fs, w1, w2, pe):
    phi_h = w1.shape[-1]
    pe_bias = jnp.einsum('ld,lde->e', pe, w1)
    hid = jax.nn.gelu(fs[:, :, :-1, :phi_h] + fs[:, :, 1:, phi_h:] + pe_bias)
    b, G, n, _ = hid.shape
    assert (b * G) % 8 == 0
    return pmm(hid.reshape(b * G * n, phi_h), w2, tm=8 * n).reshape(b, G, n, w2.shape[-1])


NSA_TQ = 128
NSA_TK = 256
CB_PAD = 128


def _nt_dot(a, b):
    return lax.dot_general(a, b, (((1,), (1,)), ((), ())), preferred_element_type=F32)


def _split3_dot(x, w):
    hi = x.astype(BF16)
    r1 = x - hi.astype(F32)
    mid = r1.astype(BF16)
    lo = (r1 - mid.astype(F32)).astype(BF16)
    return (jnp.dot(hi, w, preferred_element_type=F32) + jnp.dot(mid, w, preferred_element_type=F32)
            + jnp.dot(lo, w, preferred_element_type=F32))


def _flash_tiles(q4t, k_ref, vt_ref, lo, hi, bias_fn):
    cols = q4t.shape[1]
    last_k0 = k_ref.shape[1] - NSA_TK

    def scores(jt):
        k0 = jt * NSA_TK
        ka = pl.multiple_of(jnp.minimum(k0, last_k0), NSA_TK)
        kt = k_ref[0, pl.ds(ka, NSA_TK), :].astype(BF16)
        vtt = vt_ref[0, :, pl.ds(ka, NSA_TK)].astype(BF16)
        sm = (jnp.dot(kt, q4t, preferred_element_type=F32) * ATTN_SCALE
              + jnp.concatenate([bias_fn(k0)] * NSA_REP, axis=1))
        return sm, vtt

    def update(carry, sm, vtt):
        m, l, acc = carry
        m_new = jnp.maximum(m, jnp.max(sm, axis=0, keepdims=True))
        alpha = jnp.exp(m - m_new)
        e = jnp.exp(sm - m_new)
        l = alpha * l + jnp.sum(e, axis=0, keepdims=True)
        acc = alpha * acc + jnp.dot(vtt, e.astype(BF16), preferred_element_type=F32)
        return m_new, l, acc

    def body(i, carry):
        first, second = scores(lo + 2 * i), scores(lo + 2 * i + 1)
        return update(update(carry, *first), *second)

    init = (jnp.full((1, cols), NEG_INF, F32), jnp.zeros((1, cols), F32), jnp.zeros((HEAD_DIM, cols), F32))
    _, l, acc = lax.fori_loop(0, (hi - lo + 1) // 2, body, init)
    return acc / jnp.where(l > 0.0, l, 1.0)


def _nsa_prompt_kernel(q_ref, kc_ref, vc_ref, ks_ref, vst_ref, kw_ref, vwt_ref, gate_ref, c2s_ref, o_ref,
                       *, n_cb, n_sb):
    tq = NSA_TQ
    q0 = pl.program_id(2) * tq
    qb = q_ref[0]
    q4 = jnp.concatenate([qb[:, r * HEAD_DIM:(r + 1) * HEAD_DIM] for r in range(NSA_REP)], axis=0)

    s = _nt_dot(q4, kc_ref[0, 0]) * ATTN_SCALE
    row = lax.broadcasted_iota(jnp.int32, s.shape, 0)
    tpos = q0 + (row & (tq - 1))
    n = lax.broadcasted_iota(jnp.int32, s.shape, 1)
    cmask = (n * CMP_STRIDE + (CMP_LEN - 1) <= tpos) & (n < n_cb)
    sm = jnp.where(cmask, s, NEG_INF)
    m_c = jnp.max(sm, axis=-1, keepdims=True)
    e = jnp.where(cmask, jnp.exp(sm - m_c), 0.0)
    l_c = jnp.sum(e, axis=-1, keepdims=True)
    p = e / jnp.where(l_c > 0.0, l_c, 1.0)
    o_c = jnp.dot(p.astype(BF16), vc_ref[0, 0], preferred_element_type=F32)

    psum = p[0:tq] + p[tq:2 * tq] + p[2 * tq:3 * tq] + p[3 * tq:4 * tq]
    imp_t = _split3_dot(psum, c2s_ref[...]).T
    imp_t = imp_t[:n_sb]
    j = lax.broadcasted_iota(jnp.int32, imp_t.shape, 0)
    t = q0 + lax.broadcasted_iota(jnp.int32, imp_t.shape, 1)
    cur = jnp.right_shift(t, SEL_SHIFT)
    valid = j * SEL_BLOCK <= t
    forced = (j == 0) | (j == cur) | (j == cur - 1)
    score = jnp.where(valid, imp_t + jnp.where(forced, FORCE_BONUS, 0.0), NEG_INF)
    rank = jnp.zeros(score.shape, jnp.int32)
    for i in range(n_sb):
        si = score[i:i + 1, :]
        beats = (si > score) | ((si == score) & (i < j))
        rank = rank + jnp.where(beats, 1, 0)
    sel_t = jnp.where((rank < N_SEL) & (score > 0.5 * NEG_INF), 1.0, 0.0)
    sel_t = jnp.concatenate([sel_t, jnp.zeros((CB_PAD - n_sb, tq), F32)], axis=0).astype(BF16)

    def slc_mask(k0):
        kc = k0 + lax.broadcasted_iota(jnp.int32, (NSA_TK, CB_PAD), 0)
        jb = lax.broadcasted_iota(jnp.int32, (NSA_TK, CB_PAD), 1)
        expand = jnp.where(jb == jnp.right_shift(kc, SEL_SHIFT), 1.0, 0.0).astype(BF16)
        selx = jnp.dot(expand, sel_t, preferred_element_type=F32)
        kpos = k0 + lax.broadcasted_iota(jnp.int32, (NSA_TK, tq), 0)
        qpos = q0 + lax.broadcasted_iota(jnp.int32, (NSA_TK, tq), 1)
        return jnp.where((selx > 0.5) & (kpos <= qpos), 0.0, NEG_INF)

    def win_mask(k0):
        kpos = k0 + lax.broadcasted_iota(jnp.int32, (NSA_TK, tq), 0)
        qpos = q0 + lax.broadcasted_iota(jnp.int32, (NSA_TK, tq), 1)
        d = qpos - kpos
        return jnp.where((d >= 0) & (d < WINDOW), 0.0, NEG_INF)

    q4t = q4.astype(F32).T.astype(BF16)
    hi = (q0 + tq - 1) // NSA_TK + 1
    o_s = _flash_tiles(q4t, ks_ref, vst_ref, 0, hi, slc_mask).T
    o_w = _flash_tiles(q4t, kw_ref, vwt_ref, jnp.maximum(q0 - (WINDOW - 1), 0) // NSA_TK, hi, win_mask).T

    g = jax.nn.sigmoid(gate_ref[0, 0])
    for r in range(NSA_REP):
        rs = slice(r * tq, (r + 1) * tq)
        o = (g[:, r:r + 1] * o_c[rs] + g[:, NSA_REP + r:NSA_REP + r + 1] * o_s[rs]
             + g[:, 2 * NSA_REP + r:2 * NSA_REP + r + 1] * o_w[rs])
        o_ref[0, :, r * HEAD_DIM:(r + 1) * HEAD_DIM] = o.astype(o_ref.dtype)


def nsa_prompt_attention(q, kc, vc, k_slc, v_slc, k_win, v_win, g_nsa):
    b, T, _ = q.shape
    n_cb = kc.shape[2]
    n_sb = T // SEL_BLOCK
    assert T % NSA_TK == 0 and n_cb <= CB_PAD and n_sb <= CB_PAD and n_sb % 8 == 0
    G = NSA_KV_GROUPS

    def pack_c(x):
        return jnp.pad(x, ((0, 0), (0, 0), (0, CB_PAD - n_cb), (0, 0))).astype(BF16)

    gates = jnp.transpose(g_nsa.reshape(b, T, 3, G, NSA_REP), (0, 3, 1, 2, 4)).reshape(b, G, T, 3 * NSA_REP)
    i = np.arange(CB_PAD)[:, None] * CMP_STRIDE
    jj = np.arange(CB_PAD)[None, :] * SEL_BLOCK
    c2s = ((i < jj + SEL_BLOCK) & (i + CMP_LEN > jj) & (np.arange(CB_PAD)[:, None] < n_cb)
           & (np.arange(CB_PAD)[None, :] < n_sb))
    c2s = jnp.asarray(c2s, BF16)
    flat = lambda x: x.reshape(b, T, G * HEAD_DIM)
    flat_t = lambda x: jnp.transpose(x, (0, 2, 3, 1)).reshape(b, G * HEAD_DIM, T)
    kv_spec = pl.BlockSpec((1, T, HEAD_DIM), lambda bi, g, qi: (bi, 0, g))
    vt_spec = pl.BlockSpec((1, HEAD_DIM, T), lambda bi, g, qi: (bi, g, 0))
    c_spec = pl.BlockSpec((1, 1, CB_PAD, HEAD_DIM), lambda bi, g, qi: (bi, g, 0, 0))
    return pl.pallas_call(
        functools.partial(_nsa_prompt_kernel, n_cb=n_cb, n_sb=n_sb),
        grid=(b, G, T // NSA_TQ),
        in_specs=[pl.BlockSpec((1, NSA_TQ, NSA_REP * HEAD_DIM), lambda bi, g, qi: (bi, qi, g)),
                  c_spec, c_spec, kv_spec, vt_spec, kv_spec, vt_spec,
                  pl.BlockSpec((1, 1, NSA_TQ, 3 * NSA_REP), lambda bi, g, qi: (bi, g, qi, 0)),
                  pl.BlockSpec((CB_PAD, CB_PAD), lambda bi, g, qi: (0, 0))],
        out_specs=pl.BlockSpec((1, NSA_TQ, NSA_REP * HEAD_DIM), lambda bi, g, qi: (bi, qi, g)),
        out_shape=jax.ShapeDtypeStruct((b, T, NSA_HEADS * HEAD_DIM), BF16),
        compiler_params=pltpu.CompilerParams(
            dimension_semantics=("parallel", "parallel", "arbitrary"),
            vmem_limit_bytes=V7X_VMEM_LIMIT_BYTES),
        name="nsa_prompt",
    )(q, pack_c(kc), pack_c(vc), flat(k_slc), flat_t(v_slc), flat(k_win), flat_t(v_win), gates, c2s)


SSD_GW = SSM_REP * SSM_HEAD_DIM
DT_PAD = 128


def _split3(x):
    hi = x.astype(BF16)
    r1 = x - hi.astype(F32)
    mid = r1.astype(BF16)
    lo = (r1 - mid.astype(F32)).astype(BF16)
    return hi, mid, lo


def _split2_dot(x, w):
    hi = x.astype(BF16)
    lo = (x - hi.astype(F32)).astype(BF16)
    return jnp.dot(hi, w, preferred_element_type=F32) + jnp.dot(lo, w, preferred_element_type=F32)


def _exact_left(w, x):
    hi, mid, lo = _split3(x)
    return (jnp.dot(w, hi, preferred_element_type=F32) + jnp.dot(w, mid, preferred_element_type=F32)
            + jnp.dot(w, lo, preferred_element_type=F32))


def _ssd_prompt_kernel(xbc_ref, z_ref, dt_ref, cw_ref, cb_ref, dtb_ref, aneg_ref, dsk_ref, nw_ref,
                       expand_ref, tril_ref, y_ref, hfin_ref, xprev_ref, ht_ref):
    c = pl.program_id(1)
    Q = SSD_CHUNK

    @pl.when(c == 0)
    def _():
        xprev_ref[...] = jnp.zeros_like(xprev_ref)
        ht_ref[...] = jnp.zeros_like(ht_ref)

    x = xbc_ref[0]
    xp = xprev_ref[...]
    rowi = lax.broadcasted_iota(jnp.int32, x.shape, 0)
    acc = jnp.broadcast_to(cb_ref[...], x.shape)
    for k in range(CONV_W):
        s = CONV_W - 1 - k
        if s == 0:
            xs_k = x
        else:
            xs_k = jnp.where(rowi >= s, pltpu.roll(x, s, 0), pltpu.roll(xp, s, 0))
        acc = acc + xs_k * cw_ref[k:k + 1, :]
    xprev_ref[...] = x
    xc = acc * jax.nn.sigmoid(acc)
    xs = xc[:, :D_INNER]

    lane = lax.broadcasted_iota(jnp.int32, (Q, DT_PAD), 1)
    dtin = dt_ref[0] + dtb_ref[...]
    dt = jnp.maximum(dtin, 0.0) + jnp.log1p(jnp.exp(-jnp.abs(dtin)))
    dt = jnp.where(lane < SSM_HEADS, dt, 0.0)
    acum = _exact_left(tril_ref[...], dt * aneg_ref[...])
    acum_t = acum.T
    eacum = jnp.exp(acum)
    decay = jnp.exp(acum[Q - 1:Q, :] - acum) * dt
    expand = expand_ref[...]
    dt_e = _split2_dot(dt, expand)
    eacum_e = _split2_dot(eacum, expand)
    decay_e = _split2_dot(decay, expand)
    dtx = (dt_e * xs).astype(BF16)
    dxs = (decay_e * xs).astype(BF16)

    li = lax.broadcasted_iota(jnp.int32, (Q, Q), 0)
    si = lax.broadcasted_iota(jnp.int32, (Q, Q), 1)
    causal = li >= si
    half = lax.broadcasted_iota(jnp.int32, (Q, 2 * SSM_HEAD_DIM), 1) < SSM_HEAD_DIM

    for g in range(SSM_GROUPS):
        gs = slice(g * SSD_GW, (g + 1) * SSD_GW)
        bg = xc[:, D_INNER + g * D_STATE:D_INNER + (g + 1) * D_STATE]
        cg = xc[:, D_INNER + SSM_GROUPS * D_STATE + g * D_STATE:D_INNER + SSM_GROUPS * D_STATE + (g + 1) * D_STATE]
        bg16, cg16 = bg.astype(BF16), cg.astype(BF16)
        cbm = _nt_dot(cg16, bg16)
        ht = ht_ref[g]
        y = jnp.dot(cg16, ht.astype(BF16), preferred_element_type=F32) * eacum_e[:, gs]
        pieces = []
        for i in range(SSM_REP // 2):
            ws = []
            for h in (g * SSM_REP + 2 * i, g * SSM_REP + 2 * i + 1):
                seg = acum[:, h:h + 1] - acum_t[h:h + 1, :]
                ws.append((cbm * jnp.exp(jnp.where(causal, seg, NEG_INF))).astype(BF16))
            dpair = dtx[:, g * SSD_GW + i * 2 * SSM_HEAD_DIM:g * SSD_GW + (i + 1) * 2 * SSM_HEAD_DIM]
            pieces.append(jnp.where(half, jnp.dot(ws[0], dpair, preferred_element_type=F32),
                                    jnp.dot(ws[1], dpair, preferred_element_type=F32)))
        y = jnp.concatenate(pieces, axis=1) + y + dsk_ref[:, gs] * xs[:, gs]
        zg = z_ref[0, :, gs]
        y = y * (zg * jax.nn.sigmoid(zg))
        y = y * lax.rsqrt(jnp.mean(y * y, axis=-1, keepdims=True) + RMS_EPS)
        y_ref[0, :, gs] = (y * nw_ref[:, gs]).astype(y_ref.dtype)
        ht_ref[g] = eacum_e[Q - 1:Q, gs] * ht + jnp.dot(bg.T.astype(BF16), dxs[:, gs], preferred_element_type=F32)

    @pl.when(c == pl.num_programs(1) - 1)
    def _():
        for g in range(SSM_GROUPS):
            hfin_ref[0, g * SSD_GW:(g + 1) * SSD_GW, :] = ht_ref[g].T


def ssd_prompt(xbc, z, dt_raw, conv_w, conv_b, dt_bias, a_log, d_skip, norm_w):
    b, T, _ = xbc.shape
    assert T % SSD_CHUNK == 0 and dt_raw.shape[-1] == DT_PAD
    pad = lambda v: jnp.pad(v.astype(F32), (0, DT_PAD - SSM_HEADS)).reshape(1, DT_PAD)
    head_of = np.arange(D_INNER) // SSM_HEAD_DIM
    expand = jnp.asarray(np.arange(DT_PAD)[:, None] == head_of[None, :], BF16)
    tril = jnp.asarray(np.tril(np.ones((SSD_CHUNK, SSD_CHUNK))), BF16)
    dsk = jnp.repeat(d_skip.astype(F32), SSM_HEAD_DIM).reshape(1, D_INNER)
    const = lambda shape: pl.BlockSpec(shape, lambda bi, c: (0,) * len(shape))
    y, hfin = pl.pallas_call(
        _ssd_prompt_kernel,
        grid=(b, T // SSD_CHUNK),
        in_specs=[pl.BlockSpec((1, SSD_CHUNK, CONV_DIM), lambda bi, c: (bi, c, 0)),
                  pl.BlockSpec((1, SSD_CHUNK, D_INNER), lambda bi, c: (bi, c, 0)),
                  pl.BlockSpec((1, SSD_CHUNK, DT_PAD), lambda bi, c: (bi, c, 0)),
                  const((CONV_W, CONV_DIM)), const((1, CONV_DIM)), const((1, DT_PAD)), const((1, DT_PAD)),
                  const((1, D_INNER)), const((1, D_INNER)), const((DT_PAD, D_INNER)),
                  const((SSD_CHUNK, SSD_CHUNK))],
        out_specs=[pl.BlockSpec((1, SSD_CHUNK, D_INNER), lambda bi, c: (bi, c, 0)),
                   pl.BlockSpec((1, D_INNER, D_STATE), lambda bi, c: (bi, 0, 0))],
        out_shape=[jax.ShapeDtypeStruct((b, T, D_INNER), BF16),
                   jax.ShapeDtypeStruct((b, D_INNER, D_STATE), F32)],
        scratch_shapes=[pltpu.VMEM((SSD_CHUNK, CONV_DIM), F32),
                        pltpu.VMEM((SSM_GROUPS, D_STATE, SSD_GW), F32)],
        compiler_params=pltpu.CompilerParams(dimension_semantics=("parallel", "arbitrary"),
                                             vmem_limit_bytes=V7X_VMEM_LIMIT_BYTES),
        name="ssd_prompt",
    )(xbc, z, dt_raw, conv_w.astype(F32), conv_b.astype(F32).reshape(1, CONV_DIM), pad(dt_bias),
      pad(-jnp.exp(a_log.astype(F32))), dsk, norm_w.astype(F32).reshape(1, D_INNER), expand, tril)
    return y, hfin.reshape(b, SSM_HEADS, SSM_HEAD_DIM, D_STATE)


MOE_TM = 1024
MOE_TN = 512
MOE_TK = 1024


def _moe_up_kernel(te_ref, nu_ref, x_ref, wg_ref, wu_ref, h_ref):
    @pl.when(pl.program_id(0) < nu_ref[0])
    def _():
        x = x_ref[...]
        g = jnp.dot(x, wg_ref[0].astype(BF16), preferred_element_type=F32)
        u = jnp.dot(x, wu_ref[0].astype(BF16), preferred_element_type=F32)
        h_ref[...] = (g * jax.nn.sigmoid(g) * u).astype(h_ref.dtype)


def _moe_down_kernel(te_ref, nu_ref, h_ref, wd_ref, gw_ref, y_ref):
    k = pl.program_id(1)

    @pl.when(pl.program_id(0) < nu_ref[0])
    def _():
        part = jnp.dot(h_ref[...], wd_ref[0].astype(BF16), preferred_element_type=F32)

        @pl.when(k == 0)
        def _():
            y_ref[...] = part

        @pl.when(k > 0)
        def _():
            y_ref[...] += part

        @pl.when(k == pl.num_programs(1) - 1)
        def _():
            y_ref[...] = y_ref[...] * gw_ref[...]


def moe_grouped(x_sorted, row_w, tile_expert, n_used, wg, wu, wd, tm):
    P, D = x_sorted.shape
    E, _, FF = wg.shape
    tn = min(MOE_TN, FF)
    tk = min(MOE_TK, FF)
    assert P % tm == 0 and FF % tn == 0 and FF % tk == 0
    nt, nj, nk = P // tm, FF // tn, FF // tk

    def row(i, nu):
        return jnp.minimum(i, nu[0] - 1)

    def col(i, j, nu, last):
        return jnp.where(i < nu[0], j, last)

    h = pl.pallas_call(
        _moe_up_kernel,
        grid_spec=pltpu.PrefetchScalarGridSpec(
            num_scalar_prefetch=2, grid=(nt, nj),
            in_specs=[pl.BlockSpec((tm, D), lambda i, j, te, nu: (row(i, nu), 0)),
                      pl.BlockSpec((1, D, tn), lambda i, j, te, nu: (te[row(i, nu)], 0, col(i, j, nu, nj - 1))),
                      pl.BlockSpec((1, D, tn), lambda i, j, te, nu: (te[row(i, nu)], 0, col(i, j, nu, nj - 1)))],
            out_specs=pl.BlockSpec((tm, tn), lambda i, j, te, nu: (row(i, nu), col(i, j, nu, nj - 1)))),
        out_shape=jax.ShapeDtypeStruct((P, FF), BF16),
        compiler_params=pltpu.CompilerParams(dimension_semantics=("arbitrary", "arbitrary"),
                                             vmem_limit_bytes=V7X_VMEM_LIMIT_BYTES),
        name="moe_up",
    )(tile_expert, n_used, x_sorted, wg, wu)
    return pl.pallas_call(
        _moe_down_kernel,
        grid_spec=pltpu.PrefetchScalarGridSpec(
            num_scalar_prefetch=2, grid=(nt, nk),
            in_specs=[pl.BlockSpec((tm, tk), lambda i, k, te, nu: (row(i, nu), col(i, k, nu, nk - 1))),
                      pl.BlockSpec((1, tk, D), lambda i, k, te, nu: (te[row(i, nu)], col(i, k, nu, nk - 1), 0)),
                      pl.BlockSpec((tm, 1), lambda i, k, te, nu: (row(i, nu), 0))],
            out_specs=pl.BlockSpec((tm, D), lambda i, k, te, nu: (row(i, nu), 0))),
        out_shape=jax.ShapeDtypeStruct((P, D), F32),
        compiler_params=pltpu.CompilerParams(dimension_semantics=("arbitrary", "arbitrary"),
                                             vmem_limit_bytes=V7X_VMEM_LIMIT_BYTES),
        name="moe_down",
    )(tile_expert, n_used, h, wd, row_w)


def moe_topk(u, w_router, wg, wu, wd, tm=MOE_TM):
    N, D = u.shape
    E = wg.shape[0]
    logits = jnp.dot(u, w_router, precision=lax.Precision.HIGHEST)
    top_l, top_i = lax.top_k(logits, TOP_K)
    top_w = jax.nn.softmax(top_l, axis=-1)
    flat_e = top_i.reshape(-1)
    onehot = (flat_e[:, None] == jnp.arange(E, dtype=flat_e.dtype)[None, :]).astype(jnp.int32)
    csum = jnp.cumsum(onehot, axis=0)
    rank = jnp.take_along_axis(csum, flat_e[:, None], axis=1)[:, 0] - 1
    tiles_per_e = (csum[-1] + tm - 1) // tm
    tile_end = jnp.cumsum(tiles_per_e)
    dest = (tile_end - tiles_per_e)[flat_e] * tm + rank
    nt = (TOP_K * N + E * (tm - 1)) // tm + 1
    P = nt * tm
    tile_expert = jnp.minimum(jnp.searchsorted(tile_end, jnp.arange(nt), side='right'), E - 1).astype(jnp.int32)
    n_used = tile_end[-1:].astype(jnp.int32)
    row_token = jnp.zeros((P,), jnp.int32).at[dest].set(jnp.arange(TOP_K * N, dtype=jnp.int32) // TOP_K)
    row_w = jnp.zeros((P, 1), F32).at[dest, 0].set(top_w.reshape(-1))
    x_sorted = u.astype(BF16).at[row_token].get(mode='promise_in_bounds')
    y_sorted = moe_grouped(x_sorted, row_w, tile_expert, n_used, wg, wu, wd, tm)
    dest_k = dest.reshape(N, TOP_K)
    y = y_sorted.at[dest_k[:, 0]].get(mode='promise_in_bounds')
    for k in range(1, TOP_K):
        y = y + y_sorted.at[dest_k[:, k]].get(mode='promise_in_bounds')
    return y


def rmsnorm(x, w):
    xf = x.astype(F32)
    y = xf * lax.rsqrt(jnp.mean(xf * xf, axis=-1, keepdims=True) + RMS_EPS)
    return (y * w.astype(F32)).astype(x.dtype)


def masked_softmax(s, mask):
    p = jax.nn.softmax(jnp.where(mask, s, NEG_INF), axis=-1)
    return p * mask


def rotary(x, pos):
    half = ROT_DIM // 2
    inv = ROPE_THETA ** (-jnp.arange(half, dtype=F32) / half)
    ang = pos.astype(F32)[:, None] * inv[None, :]
    ang = ang.reshape((1, pos.shape[0]) + (1,) * (x.ndim - 3) + (half,))
    cos, sin = jnp.cos(ang), jnp.sin(ang)
    xr = x[..., :ROT_DIM].astype(F32)
    x1, x2 = xr[..., :half], xr[..., half:]
    rot = jnp.concatenate([x1 * cos - x2 * sin, x2 * cos + x1 * sin], axis=-1)
    return jnp.concatenate([rot.astype(x.dtype), x[..., ROT_DIM:]], axis=-1)


def split_in(proj):
    cuts = np.cumsum(IN_SIZES)[:-1].tolist()
    return jnp.split(proj, cuts, axis=-1)


def ssd_chunked(x, dt, a, bm, cm, h0):
    b, T = x.shape[:2]
    q = SSD_CHUNK if T >= SSD_CHUNK else T
    nc = -(-T // q)
    pad = nc * q - T

    def chunks(v):
        v = jnp.pad(v.astype(F32), [(0, 0), (0, pad)] + [(0, 0)] * (v.ndim - 2))
        return jnp.moveaxis(v.reshape((b, nc, q) + v.shape[2:]), 1, 0)

    causal = jnp.tril(jnp.ones((q, q), bool))[None, :, :, None, None]

    def step(h, inp):
        xc, dtc, bc, cc = inp
        acum = jnp.cumsum(dtc * a, axis=1)
        seg = acum[:, :, None] - acum[:, None, :]
        lmat = jnp.exp(jnp.where(causal, seg, -jnp.inf))
        cb = jnp.einsum('blgn,bsgn->blsg', cc, bc)
        y_diag = jnp.einsum('blsg,blsgr,bsgrp->blgrp', cb, lmat, dtc[..., None] * xc)
        y_off = jnp.einsum('blgn,bgrpn->blgrp', cc, h) * jnp.exp(acum)[..., None]
        decay = jnp.exp(acum[:, -1:] - acum) * dtc
        h_new = jnp.exp(acum[:, -1])[..., None, None] * h + jnp.einsum('bsgn,bsgr,bsgrp->bgrpn', bc, decay, xc)
        return h_new, y_diag + y_off

    h_fin, ys = lax.scan(step, h0.astype(F32), (chunks(x), chunks(dt), chunks(bm), chunks(cm)))
    y = jnp.moveaxis(ys, 0, 1).reshape((b, nc * q) + x.shape[2:])[:, :T]
    return y, h_fin


def ssm_branch(z, xbc, dt_raw, conv0, h0, conv_w, conv_b, dt_bias, a_log, d_skip, norm_w):
    b, T, _ = xbc.shape
    xpad = jnp.concatenate([conv0.astype(xbc.dtype), xbc], axis=1)
    acc = conv_b
    for k in range(CONV_W):
        acc = acc + xpad[:, k:k + T] * conv_w[k]
    xbc_c = jax.nn.silu(acc)
    new_conv = xpad[:, T:]
    xs, bm, cm = jnp.split(xbc_c, [D_INNER, D_INNER + SSM_GROUPS * D_STATE], axis=-1)
    xs = xs.reshape(b, T, SSM_GROUPS, SSM_REP, SSM_HEAD_DIM)
    bm = bm.reshape(b, T, SSM_GROUPS, D_STATE)
    cm = cm.reshape(b, T, SSM_GROUPS, D_STATE)
    dt = jax.nn.softplus(dt_raw.astype(F32) + dt_bias.astype(F32)).reshape(b, T, SSM_GROUPS, SSM_REP)
    a = -jnp.exp(a_log.astype(F32)).reshape(SSM_GROUPS, SSM_REP)
    h0 = h0.reshape(b, SSM_GROUPS, SSM_REP, SSM_HEAD_DIM, D_STATE)
    y, h_fin = ssd_chunked(xs, dt, a, bm, cm, h0)
    y = y + d_skip.astype(F32).reshape(SSM_GROUPS, SSM_REP)[..., None] * xs.astype(F32)
    y = y.reshape(b, T, D_INNER) * jax.nn.silu(z.astype(F32))
    yg = y.reshape(b, T, SSM_GROUPS, D_INNER // SSM_GROUPS)
    yg = yg * lax.rsqrt(jnp.mean(yg * yg, axis=-1, keepdims=True) + RMS_EPS)
    y = yg.reshape(b, T, D_INNER) * norm_w.astype(F32)
    return y.astype(z.dtype), new_conv, h_fin.reshape(b, SSM_HEADS, SSM_HEAD_DIM, D_STATE)


def compress(rows, w1, w2, pe):
    b, L = rows.shape[:2]
    nf = L // CMP_STRIDE
    ch = rows[:, :nf * CMP_STRIDE].reshape(b, nf, CMP_STRIDE, NSA_KV_GROUPS, HEAD_DIM)
    first = jnp.einsum('bnsgd,sde->bnge', ch, w1[:CMP_STRIDE])
    second = jnp.einsum('bnsgd,sde->bnge', ch, w1[CMP_STRIDE:])
    pe_bias = jnp.einsum('ld,lde->e', pe, w1)
    hid = jax.nn.gelu(first[:, :-1] + second[:, 1:] + pe_bias)
    return jnp.einsum('bnge,ed->bngd', hid, w2)


def cmp_to_sel(n_cb, n_sb):
    i = jnp.arange(n_cb)[:, None] * CMP_STRIDE
    j = jnp.arange(n_sb)[None, :] * SEL_BLOCK
    return ((i < j + SEL_BLOCK) & (i + CMP_LEN > j)).astype(F32)


def cmp_attend(q, kc, vc, q_pos):
    n = kc.shape[1]
    s = jnp.einsum('btgrd,bngd->bgrtn', q, kc, preferred_element_type=F32) * ATTN_SCALE
    end = jnp.arange(n) * CMP_STRIDE + CMP_LEN - 1
    p = masked_softmax(s, end[None, :] <= q_pos[:, None])
    o = jnp.einsum('bgrtn,bngd->btgrd', p.astype(vc.dtype), vc)
    return o, p


def select_blocks(imp, q_pos):
    n_sb = imp.shape[-1]
    j = jnp.arange(n_sb)[None, :]
    cur = (q_pos // SEL_BLOCK)[:, None]
    valid = (j * SEL_BLOCK <= q_pos[:, None])[:, None, :]
    forced = ((j == 0) | (j == cur) | (j == cur - 1))[:, None, :]
    score = jnp.where(valid, imp + jnp.where(forced, FORCE_BONUS, 0.0), NEG_INF)
    top, idx = lax.top_k(score, min(N_SEL, n_sb))
    return idx, top > 0.5 * NEG_INF


def slc_attend(q, kb, vb, idx, sel_ok, q_pos):
    kpos = idx[..., None] * SEL_BLOCK + jnp.arange(SEL_BLOCK)
    mask = sel_ok[..., None] & (kpos <= q_pos[None, :, None, None, None])
    s = jnp.einsum('btgrd,btgkld->btgrkl', q, kb, preferred_element_type=F32) * ATTN_SCALE
    b, T, G, R, K, L = s.shape
    p = masked_softmax(s.reshape(b, T, G, R, K * L), mask.reshape(b, T, G, 1, K * L))
    return jnp.einsum('btgrkl,btgkld->btgrd', p.reshape(s.shape).astype(vb.dtype), vb)


def slc_prompt(q, k, v, idx, sel_ok, q_pos):
    b, T = q.shape[:2]
    nsb = T // SEL_BLOCK
    kblk = k.reshape(b, nsb, SEL_BLOCK, NSA_KV_GROUPS, HEAD_DIM)
    vblk = v.reshape(b, nsb, SEL_BLOCK, NSA_KV_GROUPS, HEAD_DIM)
    bi = jnp.arange(b)[:, None, None, None]
    gi = jnp.arange(NSA_KV_GROUPS)[None, None, :, None]
    nq = T // SLC_QCHUNK

    def chunked(a):
        return jnp.moveaxis(a.reshape((b, nq, SLC_QCHUNK) + a.shape[2:]), 1, 0)

    def body(args):
        qc, ic, okc, pc = args
        kb = kblk[bi, ic, :, gi, :]
        vb = vblk[bi, ic, :, gi, :]
        return slc_attend(qc, kb, vb, ic, okc, pc)

    out = lax.map(body, (chunked(q), chunked(idx), chunked(sel_ok), q_pos.reshape(nq, SLC_QCHUNK)))
    return jnp.moveaxis(out, 0, 1).reshape(q.shape)


GATHER_KB = 4


def _block_gather_kernel(blk_ref, *refs):
    G = NSA_KV_GROUPS
    n = GATHER_KB * G
    k_refs, v_refs, ko_ref, vo_ref = refs[:n], refs[n:2 * n], refs[2 * n], refs[2 * n + 1]
    for kk in range(GATHER_KB):
        for g in range(G):
            ko_ref[0, g, kk] = k_refs[kk * G + g][0, 0, pl.ds(g, SEL_BLOCK, stride=G), :]
            vo_ref[0, g, kk] = v_refs[kk * G + g][0, 0, pl.ds(g, SEL_BLOCK, stride=G), :]


def gather_pool_blocks(pool_k, pool_v, layer, blk):
    b, G, K = blk.shape
    assert K % GATHER_KB == 0 and G == NSA_KV_GROUPS
    view = lambda p: p.reshape(p.shape[0], p.shape[1] * (PAGE_SIZE // SEL_BLOCK), SEL_BLOCK * G, HEAD_DIM)

    def spec(kk, g):
        return pl.BlockSpec((1, 1, SEL_BLOCK * G, HEAD_DIM),
                            lambda bi, k, ids: (layer, ids[(bi * G + g) * K + k * GATHER_KB + kk], 0, 0))

    specs = [spec(kk, g) for kk in range(GATHER_KB) for g in range(G)]
    out_spec = pl.BlockSpec((1, G, GATHER_KB, SEL_BLOCK, HEAD_DIM), lambda bi, k, ids: (bi, 0, k, 0, 0))
    out_shape = jax.ShapeDtypeStruct((b, G, K, SEL_BLOCK, HEAD_DIM), pool_k.dtype)
    return pl.pallas_call(
        _block_gather_kernel,
        grid_spec=pltpu.PrefetchScalarGridSpec(
            num_scalar_prefetch=1, grid=(b, K // GATHER_KB),
            in_specs=specs * 2,
            out_specs=[out_spec, out_spec]),
        out_shape=[out_shape, out_shape],
        compiler_params=pltpu.CompilerParams(dimension_semantics=("parallel", "arbitrary"),
                                             vmem_limit_bytes=V7X_VMEM_LIMIT_BYTES),
        name="block_gather",
    )(blk.reshape(-1), *([view(pool_k)] * len(specs)), *([view(pool_v)] * len(specs)))


def gather_selected(from_past, new_rows, idx, past):
    b, S = new_rows.shape[:2]
    n_past_blk = past // SEL_BLOCK
    n_new_blk = -(-S // SEL_BLOCK)
    bi = jnp.arange(b)[:, None, None, None]
    gi = jnp.arange(NSA_KV_GROUPS)[None, None, :, None]
    new_blk = jnp.pad(new_rows, ((0, 0), (0, n_new_blk * SEL_BLOCK - S), (0, 0), (0, 0)))
    new_blk = new_blk.reshape(b, n_new_blk, SEL_BLOCK, NSA_KV_GROUPS, HEAD_DIM)
    jn = jnp.clip(idx - n_past_blk, 0, n_new_blk - 1)
    from_new = new_blk[bi, jn, :, gi, :]
    return jnp.where((idx < n_past_blk)[..., None, None], from_past, from_new.astype(from_past.dtype))


def window_attend(q, k, v, q_pos, k_pos):
    s = jnp.einsum('btgrd,bsgd->bgrts', q, k, preferred_element_type=F32) * ATTN_SCALE
    diff = q_pos[:, None] - k_pos[None, :]
    mask = (diff >= 0) & (diff < WINDOW) & (k_pos[None, :] >= 0)
    p = masked_softmax(s, mask)
    return jnp.einsum('bgrts,bsgd->btgrd', p.astype(v.dtype), v)


def win_prompt(q, k, v):
    b, T = q.shape[:2]
    nq = T // WIN_QBLOCK
    span = WIN_QBLOCK + WINDOW
    kp = jnp.pad(k, ((0, 0), (WINDOW, 0), (0, 0), (0, 0)))
    vp = jnp.pad(v, ((0, 0), (WINDOW, 0), (0, 0), (0, 0)))
    qb = jnp.moveaxis(q.reshape((b, nq, WIN_QBLOCK) + q.shape[2:]), 1, 0)

    def body(args):
        i, qc = args
        start = i * WIN_QBLOCK
        kc = lax.dynamic_slice_in_dim(kp, start, span, axis=1)
        vc = lax.dynamic_slice_in_dim(vp, start, span, axis=1)
        q_pos = start + jnp.arange(WIN_QBLOCK)
        k_pos = start - WINDOW + jnp.arange(span)
        return window_attend(qc, kc, vc, q_pos, k_pos)

    out = lax.map(body, (jnp.arange(nq), qb))
    return jnp.moveaxis(out, 0, 1).reshape(q.shape)


def combine_nsa(gate, o_cmp, o_slc, o_win, dtype):
    b, T = o_cmp.shape[:2]
    o = gate[:, :, 0] * o_cmp + gate[:, :, 1] * o_slc + gate[:, :, 2] * o_win
    return o.reshape(b, T, NSA_HEADS * HEAD_DIM).astype(dtype)


def nsa_prompt(q, kvs, g_nsa, phi):
    phi1_k, phi2_k, pe_k, phi1_v, phi2_v, pe_v = phi
    b, T, _ = q.shape
    cos_t, sin_t = rope_tables(jnp.arange(T, dtype=jnp.int32))
    heads = lambda x: x.reshape(b, T, NSA_KV_GROUPS, HEAD_DIM)
    rot = lambda x, dt: rope_rows(x.reshape(b * T, x.shape[-1]), cos_t, sin_t, dt).reshape(x.shape)
    q = rot(q, BF16)
    k_cmp, k_slc, k_win = heads(rot(kvs[0], F32)), heads(rot(kvs[2], F32)), heads(rot(kvs[4], F32))
    v_cmp, v_slc, v_win = heads(kvs[1]), heads(kvs[3]), heads(kvs[5])
    assert T % (PAGE_SIZE * CMP_PAGES) == 0
    pages = jnp.arange(b * T // PAGE_SIZE, dtype=jnp.int32)
    as_pool = lambda x: x.reshape(1, b * T // PAGE_SIZE, PAGE_SIZE * NSA_KV_GROUPS, HEAD_DIM)
    kc = compress_finish(compress_first(as_pool(k_cmp), 0, pages, b, phi1_k), phi1_k, phi2_k, pe_k)
    vc = compress_finish(compress_first(as_pool(v_cmp), 0, pages, b, phi1_v), phi1_v, phi2_v, pe_v)
    o = nsa_prompt_attention(q, kc, vc, k_slc, v_slc, k_win, v_win, g_nsa)
    wb = min(WINDOW, T)
    return o, (k_cmp, v_cmp, k_slc, v_slc, k_win[:, T - wb:], v_win[:, T - wb:])


def nsa_sample(q, kv, g_nsa, phi, layer, pool_k_cmp, pool_v_cmp, ck_slc, cv_slc, ck_win, cv_win, page_table):
    phi1_k, phi2_k, pe_k, phi1_v, phi2_v, pe_v = phi
    b, S = q.shape[:2]
    assert S < CMP_STRIDE
    gate = jax.nn.sigmoid(g_nsa.astype(F32)).reshape(b, S, 3, NSA_KV_GROUPS, NSA_REP, 1)
    past = page_table.shape[1] * PAGE_SIZE
    pos = past + jnp.arange(S, dtype=jnp.int32)
    q = rotary(q, pos)
    k_cmp, k_slc, k_win = rotary(kv[:, :, 0], pos), rotary(kv[:, :, 2], pos), rotary(kv[:, :, 4], pos)
    v_cmp, v_slc, v_win = kv[:, :, 1], kv[:, :, 3], kv[:, :, 5]

    def compress_past(pool, w1, w2, pe):
        pool = pool.reshape(pool.shape[:2] + (PAGE_SIZE * NSA_KV_GROUPS, HEAD_DIM))
        blocks = compress_finish(compress_first(pool, layer, page_table.reshape(-1), b, w1), w1, w2, pe)
        return jnp.transpose(blocks, (0, 2, 1, 3))

    kc = compress_past(pool_k_cmp, phi1_k, phi2_k, pe_k)
    vc = compress_past(pool_v_cmp, phi1_v, phi2_v, pe_v)
    wb = ck_win.shape[1]
    k_all = jnp.concatenate([ck_win.astype(k_win.dtype), k_win], axis=1)
    v_all = jnp.concatenate([cv_win.astype(v_win.dtype), v_win], axis=1)
    k_pos = past - wb + jnp.arange(wb + S)
    with jax.default_matmul_precision("highest"):
        o_cmp, p_cmp = cmp_attend(q, kc, vc, pos)
        imp = jnp.einsum('bgrtn,nj->btgj', p_cmp, cmp_to_sel(kc.shape[1], -(-(past + S) // SEL_BLOCK)))
        idx, sel_ok = select_blocks(imp, pos)
        bpp = PAGE_SIZE // SEL_BLOCK
        jp = jnp.minimum(idx, past // SEL_BLOCK - 1)[:, 0]
        blk = jnp.take_along_axis(page_table, (jp // bpp).reshape(b, -1), axis=1).reshape(jp.shape) * bpp + jp % bpp
        kb, vb = gather_pool_blocks(ck_slc, cv_slc, layer, blk.astype(jnp.int32))
        kb = gather_selected(kb[:, None], k_slc, idx, past)
        vb = gather_selected(vb[:, None], v_slc, idx, past)
        o_slc = slc_attend(q, kb, vb, idx, sel_ok, pos)
        o_win = window_attend(q, k_all, v_all, pos, k_pos)
    o = combine_nsa(gate, o_cmp, o_slc, o_win, q.dtype)
    return o, (k_cmp, v_cmp, k_slc, v_slc, k_all[:, -wb:], v_all[:, -wb:])


def token_mixer(h, conv0, ssm0, nsa_fn, norm_w, w_in, conv_w, conv_b, dt_bias, a_log, d_skip,
                ssm_norm_w, w_br_ssm, w_br_nsa, w_out):
    b, T, _ = h.shape
    sample = conv0 is not None
    u = rmsnorm(h, norm_w)
    if not sample:
        u = u.astype(BF16)
    proj = functools.partial(mm, precise=sample)
    o = np.cumsum((0,) + IN_SIZES)
    z, xbc, q, g_nsa, g_merge = (proj(u, w_in[:, o[i]:o[i + 1]]) for i in (0, 1, 3, 5, 6))
    dt_pad = proj(u, w_in[:, o[2]:o[2] + DT_PAD])
    if not sample:
        y_ssm, ssm_new = ssd_prompt(xbc, z, dt_pad, conv_w, conv_b, dt_bias, a_log, d_skip, ssm_norm_w)
        conv_new = xbc[:, T - (CONV_W - 1):]
        kvs = [proj(u, w_in[:, o[4] + i * KV_WIDTH:o[4] + (i + 1) * KV_WIDTH]) for i in range(6)]
        o_nsa, nsa_state = nsa_fn(q, kvs, g_nsa)
    else:
        with jax.default_matmul_precision("highest"):
            y_ssm, conv_new, ssm_new = ssm_branch(z, xbc, dt_pad[..., :SSM_HEADS], conv0, ssm0, conv_w, conv_b,
                                                  dt_bias, a_log, d_skip, ssm_norm_w)
        q = q.reshape(b, T, NSA_KV_GROUPS, NSA_REP, HEAD_DIM)
        kv = proj(u, w_in[:, o[4]:o[5]]).reshape(b, T, 6, NSA_KV_GROUPS, HEAD_DIM)
        o_nsa, nsa_state = nsa_fn(q, kv, g_nsa)
    rows = lambda x: x.reshape(b * T, x.shape[-1])
    merged = gated_merge(rows(y_ssm), w_br_ssm, rows(o_nsa), w_br_nsa, rows(g_merge), precise=sample)
    out = pmm(merged, w_out, precise=sample).reshape(b, T, D_MODEL)
    return out, nsa_state + (ssm_new.astype(h.dtype), conv_new)


def swiglu(u, wg, wu, wd, precise=False):
    lead = u.shape[:-1]
    hid = swiglu_up(u.reshape(-1, u.shape[-1]), wg, wu, precise=precise)
    return pmm(hid, wd, precise=precise).reshape(lead + (wd.shape[-1],))


def channel_mixer(hp, hs, l, norm_w, w_gate, w_up, w_down, w_router, w_gate_e, w_up_e, w_down_e):
    up, us = rmsnorm(hp, norm_w), rmsnorm(hs, norm_w)
    i = l // 2
    if l % 2 == 0:
        return (swiglu(up.astype(BF16), w_gate[i], w_up[i], w_down[i]),
                swiglu(us, w_gate[i], w_up[i], w_down[i], precise=True))
    n_p = up.shape[0] * up.shape[1]
    u_all = jnp.concatenate([up.reshape(n_p, D_MODEL), us.reshape(-1, D_MODEL)], axis=0)
    y = moe_topk(u_all, w_router[i], w_gate_e[i], w_up_e[i], w_down_e[i])
    return y[:n_p].reshape(up.shape), y[n_p:].reshape(us.shape)


def kernel(x_prompt, x_sample, cache_k_cmp, cache_v_cmp, cache_k_slc, cache_v_slc, cache_k_win, cache_v_win, state_ssm, state_conv, page_table, norm_mix, w_in, conv_w, conv_b, dt_bias, a_log, d_skip, ssm_norm, phi1_k, phi2_k, pe_k, phi1_v, phi2_v, pe_v, w_br_ssm, w_br_nsa, w_out, norm_ffn, w_gate, w_up, w_down, w_router, w_gate_e, w_up_e, w_down_e, norm_final):
    hp, hs = x_prompt, x_sample
    bp = x_prompt.shape[0]
    p_states = [[] for _ in range(8)]
    s_states = [[] for _ in range(8)]
    for l in range(DEPTH):
        mw = (norm_mix[l], w_in[l], conv_w[l], conv_b[l], dt_bias[l], a_log[l], d_skip[l],
              ssm_norm[l], w_br_ssm[l], w_br_nsa[l], w_out[l])
        phi = (phi1_k[l], phi2_k[l], pe_k[l], phi1_v[l], phi2_v[l], pe_v[l])
        ffn = (w_gate, w_up, w_down, w_router, w_gate_e, w_up_e, w_down_e)
        out, st = token_mixer(hp, None, None, functools.partial(nsa_prompt, phi=phi), *mw)
        hp = hp + out
        for i in range(8):
            p_states[i].append(st[i])
        nsa_fn = functools.partial(nsa_sample, phi=phi, layer=l, pool_k_cmp=cache_k_cmp, pool_v_cmp=cache_v_cmp,
                                   ck_slc=cache_k_slc, cv_slc=cache_v_slc,
                                   ck_win=cache_k_win[l], cv_win=cache_v_win[l], page_table=page_table)
        out, st = token_mixer(hs, state_conv[l], state_ssm[l], nsa_fn, *mw)
        hs = hs + out
        for i in range(8):
            s_states[i].append(st[i])
        fp, fs = channel_mixer(hp, hs, l, norm_ffn[l], *ffn)
        hp, hs = hp + fp, hs + fs
    y_prompt = rmsnorm(hp, norm_final)
    y_sample = rmsnorm(hs, norm_final)
    p_out = [jnp.stack(a) for a in p_states]
    s_out = [jnp.stack(a) for a in s_states]
    return (y_prompt, y_sample, *p_out, *s_out)
```
